```python
import jax, jax.numpy as jnp
from jax import lax
import numpy as np

D_MODEL = 1024
BATCH = 32
SEQ = 2048
DEPTH = 2

N_MIXERS = 2
EXPAND = 2
D_INNER = EXPAND * D_MODEL
HEAD_DIM = 64
N_HEADS = D_INNER // HEAD_DIM
Q_BLOCK = 128
NORM_EPS = 1e-6
N_FOX = (DEPTH + N_MIXERS - 1) // N_MIXERS
N_SB = DEPTH // N_MIXERS
FOX_IN = 4 * D_INNER + N_HEADS
SB_IN = 4 * D_INNER

kernel_name = "hybrid_fox_stickbreaking_adaln"


def rmsnorm(x, g):
    xf = x.astype(jnp.float32)
    y = xf * lax.rsqrt(jnp.mean(xf * xf, axis=-1, keepdims=True) + NORM_EPS)
    return (y * g.astype(jnp.float32)).astype(x.dtype)


def adaln(c, w_ada, b_ada):
    mod = jax.nn.silu(c) @ w_ada + b_ada
    shift, scale, gate = jnp.split(mod, 3, axis=-1)
    return shift[:, None, :], scale[:, None, :], gate[:, None, :]


def split_heads(t):
    b, s, _ = t.shape
    return t.reshape(b, s, N_HEADS, HEAD_DIM).transpose(0, 2, 1, 3)


def merge_heads(t):
    b, h, s, d = t.shape
    return t.transpose(0, 2, 1, 3).reshape(b, s, h * d)


def forgetting_attention(q, k, v, log_f):
    seq = q.shape[2]
    cum = jnp.cumsum(log_f, axis=-1)
    sm_scale = HEAD_DIM ** -0.5
    outs = []
    for blk in range(seq // Q_BLOCK):
        q0, q1 = blk * Q_BLOCK, (blk + 1) * Q_BLOCK
        qb, kp, vp = q[:, :, q0:q1], k[:, :, :q1], v[:, :, :q1]
        logits = jnp.einsum("bhtd,bhsd->bhts", qb, kp,
                            preferred_element_type=jnp.float32) * sm_scale
        logits = logits + cum[:, :, q0:q1, None] - cum[:, :, None, :q1]
        t_idx = jnp.arange(q0, q1)[:, None]
        s_idx = jnp.arange(q1)[None, :]
        logits = jnp.where(s_idx <= t_idx, logits, -jnp.inf)
        p = jax.nn.softmax(logits, axis=-1)
        outs.append(jnp.einsum("bhts,bhsd->bhtd", p.astype(vp.dtype), vp))
    return jnp.concatenate(outs, axis=2)


def stick_breaking_attention(q, k, v):
    seq = q.shape[2]
    sm_scale = HEAD_DIM ** -0.5
    outs = []
    for blk in range(seq // Q_BLOCK):
        q0, q1 = blk * Q_BLOCK, (blk + 1) * Q_BLOCK
        qb, kp, vp = q[:, :, q0:q1], k[:, :, :q1], v[:, :, :q1]
        z = jnp.einsum("bhtd,bhsd->bhts", qb, kp,
                       preferred_element_type=jnp.float32) * sm_scale
        t_idx = jnp.arange(q0, q1)[:, None]
        s_idx = jnp.arange(q1)[None, :]
        strict = s_idx < t_idx
        log_beta = jax.nn.log_sigmoid(z)
        log_keep = jnp.where(strict, jax.nn.log_sigmoid(-z), 0.0)
        after = lax.cumsum(log_keep, axis=3, reverse=True) - log_keep
        a = jnp.where(strict, jnp.exp(log_beta + after), 0.0)
        outs.append(jnp.einsum("bhts,bhsd->bhtd", a.astype(vp.dtype), vp))
    return jnp.concatenate(outs, axis=2)


def fox_layer(x, c, norm_g, w_ada, b_ada, w_in, b_f, w_out):
    shift, scale, gate = adaln(c, w_ada, b_ada)
    h = rmsnorm(x, norm_g) * (1.0 + scale) + shift
    proj = h @ w_in
    q, k, v, zg = (proj[..., i * D_INNER:(i + 1) * D_INNER] for i in range(4))
    f_logit = proj[..., 4 * D_INNER:] + b_f
    log_f = jax.nn.log_sigmoid(f_logit.astype(jnp.float32)).transpose(0, 2, 1)
    o = forgetting_attention(split_heads(q), split_heads(k), split_heads(v), log_f)
    y = (merge_heads(o) * jax.nn.silu(zg)) @ w_out
    return x + gate * y


def sb_layer(x, c, norm_g, w_ada, b_ada, w_in, w_out):
    shift, scale, gate = adaln(c, w_ada, b_ada)
    h = rmsnorm(x, norm_g) * (1.0 + scale) + shift
    proj = h @ w_in
    q, k, v, zg = (proj[..., i * D_INNER:(i + 1) * D_INNER] for i in range(4))
    o = stick_breaking_attention(split_heads(q), split_heads(k), split_heads(v))
    y = (merge_heads(o) * jax.nn.silu(zg)) @ w_out
    return x + gate * y


def setup_inputs(seed: int = 0) -> dict:
    key = jax.random.key(seed)
    ks = jax.random.split(key, 16)
    f32 = jnp.float32
    din = D_MODEL ** -0.5
    dinner = D_INNER ** -0.5
    x = jax.random.normal(ks[0], (BATCH, SEQ, D_MODEL), f32)
    c = jax.random.normal(ks[1], (BATCH, D_MODEL), f32)
    fox_norm_g = 1.0 + 0.02 * jax.random.normal(ks[2], (N_FOX, D_MODEL), f32)
    fox_w_ada = 0.5 * din * jax.random.normal(ks[3], (N_FOX, D_MODEL, 3 * D_MODEL), f32)
    fox_b_ada = 0.02 * jax.random.normal(ks[4], (N_FOX, 3 * D_MODEL), f32)
    fox_w_in = din * jax.random.normal(ks[5], (N_FOX, D_MODEL, FOX_IN), f32)
    fox_b_f = 1.0 + 0.5 * jax.random.normal(ks[6], (N_FOX, N_HEADS), f32)
    fox_w_out = dinner * jax.random.normal(ks[7], (N_FOX, D_INNER, D_MODEL), f32)
    sb_norm_g = 1.0 + 0.02 * jax.random.normal(ks[8], (N_SB, D_MODEL), f32)
    sb_w_ada = 0.5 * din * jax.random.normal(ks[9], (N_SB, D_MODEL, 3 * D_MODEL), f32)
    sb_b_ada = 0.02 * jax.random.normal(ks[10], (N_SB, 3 * D_MODEL), f32)
    sb_w_in = din * jax.random.normal(ks[11], (N_SB, D_MODEL, SB_IN), f32)
    sb_w_out = dinner * jax.random.normal(ks[12], (N_SB, D_INNER, D_MODEL), f32)
    final_norm_g = 1.0 + 0.02 * jax.random.normal(ks[13], (D_MODEL,), f32)
    return {"x": x, "c": c,
            "fox_norm_g": fox_norm_g, "fox_w_ada": fox_w_ada, "fox_b_ada": fox_b_ada,
            "fox_w_in": fox_w_in, "fox_b_f": fox_b_f, "fox_w_out": fox_w_out,
            "sb_norm_g": sb_norm_g, "sb_w_ada": sb_w_ada, "sb_b_ada": sb_b_ada,
            "sb_w_in": sb_w_in, "sb_w_out": sb_w_out,
            "final_norm_g": final_norm_g}


def reference(x, c, fox_norm_g, fox_w_ada, fox_b_ada, fox_w_in, fox_b_f, fox_w_out,
              sb_norm_g, sb_w_ada, sb_b_ada, sb_w_in, sb_w_out, final_norm_g):
    for i in range(DEPTH):
        j = i // N_MIXERS
        if i % N_MIXERS == 0:
            x = fox_layer(x, c, fox_norm_g[j], fox_w_ada[j], fox_b_ada[j],
                          fox_w_in[j], fox_b_f[j], fox_w_out[j])
        else:
            x = sb_layer(x, c, sb_norm_g[j], sb_w_ada[j], sb_b_ada[j],
                         sb_w_in[j], sb_w_out[j])
    return rmsnorm(x, final_norm_g)
```

```python
import functools

import jax
import jax.numpy as jnp
from jax import lax
from jax.experimental import pallas as pl
from jax.experimental.pallas import tpu as pltpu

D_MODEL = 1024
D_INNER = 2048
HEAD_DIM = 64
N_HEADS = D_INNER // HEAD_DIM
NORM_EPS = 1e-6
SM_SCALE = HEAD_DIM ** -0.5

LANES = 128
HEADS_PER_BLOCK = LANES // HEAD_DIM
VMEM_LIMIT = 56 * 1024 * 1024

ROW_TILE = 512
COL_CHUNK = 512
ADA_COLS = 512
TQ = 256
TK = 256

F32 = jnp.float32
BF16 = jnp.bfloat16


def _split3(x):
    hi = x.astype(BF16)
    r1 = x - hi.astype(F32)
    mid = r1.astype(BF16)
    lo = (r1 - mid.astype(F32)).astype(BF16)
    return hi, mid, lo


def _log_sigmoid(x):
    return jnp.minimum(x, 0.0) - jnp.log1p(jnp.exp(-jnp.abs(x)))


def _adaln_kernel(c_ref, w_ref, b_ref, o_ref):
    c = c_ref[...]
    s = c * jax.nn.sigmoid(c)
    o_ref[...] = jnp.dot(s, w_ref[...], preferred_element_type=F32,
                         precision=lax.Precision.HIGHEST) + b_ref[...]


def _adaln(c, w_ada, b_ada):
    b, d = c.shape
    n = w_ada.shape[1]
    return pl.pallas_call(
        _adaln_kernel,
        grid=(n // ADA_COLS,),
        in_specs=[pl.BlockSpec((b, d), lambda j: (0, 0)),
                  pl.BlockSpec((d, ADA_COLS), lambda j: (0, j)),
                  pl.BlockSpec((1, ADA_COLS), lambda j: (0, j))],
        out_specs=pl.BlockSpec((b, ADA_COLS), lambda j: (0, j)),
        out_shape=jax.ShapeDtypeStruct((b, n), F32),
        name="adaln",
    )(c, w_ada, b_ada.reshape(1, n))


def _modulated_norm(x_ref, shift_ref, scale_ref, g_ref):
    x = x_ref[0]
    y = x * lax.rsqrt(jnp.mean(x * x, axis=-1, keepdims=True) + NORM_EPS) * g_ref[...]
    return (y * (1.0 + scale_ref[0]) + shift_ref[0]).astype(BF16)


def _project(h, w_ref, out_refs):
    for o, out_ref in enumerate(out_refs):
        for cc in range(D_INNER // COL_CHUNK):
            col = o * D_INNER + cc * COL_CHUNK
            out_ref[0, :, cc * COL_CHUNK:(cc + 1) * COL_CHUNK] = jnp.dot(
                h, w_ref[:, col:col + COL_CHUNK], preferred_element_type=F32).astype(BF16)


def _inproj_sb_kernel(x_ref, shift_ref, scale_ref, g_ref, w_ref, q_ref, k_ref, v_ref, z_ref):
    h = _modulated_norm(x_ref, shift_ref, scale_ref, g_ref)
    _project(h, w_ref, (q_ref, k_ref, v_ref, z_ref))


def _inproj_fox_kernel(x_ref, shift_ref, scale_ref, g_ref, w_ref, wf_ref, bf_ref,
                       q_ref, k_ref, v_ref, z_ref, cum_ref, carry_ref):
    h = _modulated_norm(x_ref, shift_ref, scale_ref, g_ref)
    _project(h, w_ref, (q_ref, k_ref, v_ref, z_ref))

    @pl.when(pl.program_id(1) == 0)
    def _():
        carry_ref[...] = jnp.zeros_like(carry_ref)

    log_f = _log_sigmoid(jnp.dot(h, wf_ref[...], preferred_element_type=F32) + bf_ref[...])
    rows = log_f.shape[0]
    r = lax.broadcasted_iota(jnp.int32, (rows, rows), 0)
    c = lax.broadcasted_iota(jnp.int32, (rows, rows), 1)
    tri = (c <= r).astype(BF16)
    hi, mid, lo = _split3(log_f)
    cum = (jnp.dot(tri, hi, preferred_element_type=F32)
           + jnp.dot(tri, mid, preferred_element_type=F32)
           + jnp.dot(tri, lo, preferred_element_type=F32)) + carry_ref[...]
    cum_ref[0] = cum
    carry_ref[...] = cum[rows - 1:rows, :]


def _inproj(x, mod3, g, w, forget=None):
    b, s, d = x.shape
    row_spec = lambda width: pl.BlockSpec((1, ROW_TILE, width), lambda i, j: (i, j, 0))
    mod_spec = lambda k: pl.BlockSpec((1, 1, d), lambda i, j: (3 * i + k, 0, 0))
    const = lambda shape: pl.BlockSpec(shape, lambda i, j: (0,) * len(shape),
                                       pipeline_mode=pl.Buffered(1))
    in_specs = [row_spec(d), mod_spec(0), mod_spec(1), const((1, d)), const(w.shape)]
    args = [x, mod3, mod3, g.reshape(1, d), w]
    out_specs = [row_spec(D_INNER)] * 4
    out_shape = [jax.ShapeDtypeStruct((b, s, D_INNER), BF16)] * 4
    scratch = []
    kernel = _inproj_sb_kernel
    if forget is not None:
        wf, bf = forget
        in_specs += [const(wf.shape), const(bf.shape)]
        args += [wf, bf]
        out_specs.append(row_spec(LANES))
        out_shape.append(jax.ShapeDtypeStruct((b, s, LANES), F32))
        scratch = [pltpu.VMEM((1, LANES), F32)]
        kernel = _inproj_fox_kernel
    return pl.pallas_call(
        kernel,
        grid=(b, s // ROW_TILE),
        in_specs=in_specs,
        out_specs=out_specs,
        out_shape=out_shape,
        scratch_shapes=scratch,
        compiler_params=pltpu.CompilerParams(
            dimension_semantics=("parallel", "arbitrary"), vmem_limit_bytes=VMEM_LIMIT),
        name="inproj_fox" if forget is not None else "inproj_sb",
    )(*args)


def _head_masked_queries(q_ref, i):
    q = q_ref[0, pl.ds(i * TQ, TQ), :] * SM_SCALE
    lane = lax.broadcasted_iota(jnp.int32, q.shape, 1)
    zero = jnp.zeros_like(q)
    return jnp.concatenate([jnp.where(lane < HEAD_DIM, q, zero),
                            jnp.where(lane >= HEAD_DIM, q, zero)], axis=0)


def _scores(qq, k_ref, j):
    kt = k_ref[0, pl.ds(j * TK, TK), :]
    return lax.dot_general(qq, kt, (((1,), (1,)), ((), ())), preferred_element_type=F32)


def _gated_store(o_ref, z_ref, i, outs):
    lane = lax.broadcasted_iota(jnp.int32, outs[0].shape, 1)
    o = jnp.where(lane < HEAD_DIM, outs[0], outs[1])
    z = z_ref[0, pl.ds(i * TQ, TQ), :].astype(F32)
    o_ref[0, pl.ds(i * TQ, TQ), :] = (o * (z * jax.nn.sigmoid(z))).astype(o_ref.dtype)


def _fox_kernel(q_ref, k_ref, v_ref, nb_ref, z_ref, o_ref):
    def q_tile(i, _):
        qq = _head_masked_queries(q_ref, i)
        row = lax.broadcasted_iota(jnp.int32, (TQ, TK), 0)
        col = lax.broadcasted_iota(jnp.int32, (TQ, TK), 1)

        s = _scores(qq, k_ref, i)
        vt = v_ref[0, pl.ds(i * TK, TK), :]
        state = []
        for h in range(HEADS_PER_BLOCK):
            sh = s[h * TQ:(h + 1) * TQ] + nb_ref[0, 0, h:h + 1, pl.ds(i * TK, TK)]
            sh = jnp.where(col <= row, sh, -jnp.inf)
            m = jnp.max(sh, axis=-1, keepdims=True)
            p = jnp.exp(sh - m)
            l = jnp.sum(p, axis=-1, keepdims=True)
            acc = jnp.dot(p.astype(BF16), vt, preferred_element_type=F32)
            state += [m, l, acc]

        def kv_tile(j, state):
            s = _scores(qq, k_ref, j)
            vt = v_ref[0, pl.ds(j * TK, TK), :]
            new = []
            for h in range(HEADS_PER_BLOCK):
                m, l, acc = state[3 * h:3 * h + 3]
                sh = s[h * TQ:(h + 1) * TQ] + nb_ref[0, 0, h:h + 1, pl.ds(j * TK, TK)]
                m_new = jnp.maximum(m, jnp.max(sh, axis=-1, keepdims=True))
                alpha = jnp.exp(m - m_new)
                p = jnp.exp(sh - m_new)
                l = alpha * l + jnp.sum(p, axis=-1, keepdims=True)
                acc = alpha * acc + jnp.dot(p.astype(BF16), vt, preferred_element_type=F32)
                new += [m_new, l, acc]
            return tuple(new)

        state = lax.fori_loop(0, i, kv_tile, tuple(state))
        outs = [state[3 * h + 2] / state[3 * h + 1] for h in range(HEADS_PER_BLOCK)]
        _gated_store(o_ref, z_ref, i, outs)
        return 0

    lax.fori_loop(0, q_ref.shape[1] // TQ, q_tile, 0)


def _sb_kernel(q_ref, k_ref, v_ref, z_ref, tri_ref, o_ref):
    def q_tile(i, _):
        qq = _head_masked_queries(q_ref, i)
        row = lax.broadcasted_iota(jnp.int32, (TQ, TK), 0)
        col = lax.broadcasted_iota(jnp.int32, (TQ, TK), 1)
        strict = col < row
        tri = tri_ref[...]

        def tile(j, state, diagonal):
            s = _scores(qq, k_ref, j)
            vt = v_ref[0, pl.ds(j * TK, TK), :]
            new = []
            for h in range(HEADS_PER_BLOCK):
                tot, acc = state[2 * h:2 * h + 2]
                zz = s[h * TQ:(h + 1) * TQ]
                log_beta = _log_sigmoid(zz)
                log_keep = log_beta - zz
                if diagonal:
                    log_keep = jnp.where(strict, log_keep, 0.0)
                hi, mid, lo = _split3(log_keep)
                after = (jnp.dot(hi, tri, preferred_element_type=F32)
                         + jnp.dot(mid, tri, preferred_element_type=F32)
                         + jnp.dot(lo, tri, preferred_element_type=F32)) + tot
                a = jnp.exp(log_beta + after)
                if diagonal:
                    a = jnp.where(strict, a, 0.0)
                acc = acc + jnp.dot(a.astype(BF16), vt, preferred_element_type=F32)
                tot = tot + jnp.sum(log_keep, axis=-1, keepdims=True)
                new += [tot, acc]
            return tuple(new)

        zero_state = (jnp.zeros((TQ, 1), F32), jnp.zeros((TQ, LANES), F32)) * HEADS_PER_BLOCK
        state = tile(i, zero_state, True)
        state = lax.fori_loop(0, i, lambda n, st: tile(i - 1 - n, st, False), state)
        _gated_store(o_ref, z_ref, i, [state[2 * h + 1] for h in range(HEADS_PER_BLOCK)])
        return 0

    lax.fori_loop(0, q_ref.shape[1] // TQ, q_tile, 0)


def _mixer(kernel, q, k, v, z, extra, extra_spec, name):
    b, s, _ = q.shape
    seq_spec = pl.BlockSpec((1, s, LANES), lambda i, j: (i, 0, j))
    return pl.pallas_call(
        kernel,
        grid=(b, D_INNER // LANES),
        in_specs=[seq_spec, seq_spec, seq_spec] + (
            [extra_spec, seq_spec] if name == "fox" else [seq_spec, extra_spec]),
        out_specs=seq_spec,
        out_shape=jax.ShapeDtypeStruct((b, s, D_INNER), BF16),
        compiler_params=pltpu.CompilerParams(
            dimension_semantics=("parallel", "parallel"), vmem_limit_bytes=VMEM_LIMIT),
        name=name,
    )(*((q, k, v, extra, z) if name == "fox" else (q, k, v, z, extra)))


def _outproj_kernel(u_ref, w_ref, x_ref, gate_ref, o_ref):
    y = jnp.dot(u_ref[0], w_ref[...], preferred_element_type=F32)
    o_ref[0] = x_ref[0] + gate_ref[0] * y


def _outproj_final_kernel(u_ref, w_ref, x_ref, gate_ref, g_ref, o_ref):
    y = jnp.dot(u_ref[0], w_ref[...], preferred_element_type=F32)
    r = x_ref[0] + gate_ref[0] * y
    o_ref[0] = r * lax.rsqrt(jnp.mean(r * r, axis=-1, keepdims=True) + NORM_EPS) * g_ref[...]


def _outproj(u, w, x, mod3, final_g=None):
    b, s, d = x.shape
    row_spec = lambda width: pl.BlockSpec((1, ROW_TILE, width), lambda i, j: (i, j, 0))
    const = lambda shape: pl.BlockSpec(shape, lambda i, j: (0,) * len(shape),
                                       pipeline_mode=pl.Buffered(1))
    in_specs = [row_spec(D_INNER), const(w.shape), row_spec(d),
                pl.BlockSpec((1, 1, d), lambda i, j: (3 * i + 2, 0, 0))]
    args = [u, w, x, mod3]
    kernel = _outproj_kernel
    if final_g is not None:
        in_specs.append(const((1, d)))
        args.append(final_g.reshape(1, d))
        kernel = _outproj_final_kernel
    return pl.pallas_call(
        kernel,
        grid=(b, s // ROW_TILE),
        in_specs=in_specs,
        out_specs=row_spec(d),
        out_shape=jax.ShapeDtypeStruct((b, s, d), F32),
        compiler_params=pltpu.CompilerParams(
            dimension_semantics=("parallel", "parallel"), vmem_limit_bytes=VMEM_LIMIT),
        name="outproj_final" if final_g is not None else "outproj",
    )(*args)


def _fox_layer(x, c, norm_g, w_ada, b_ada, w_in, b_f, w_out, final_g):
    b, s, d = x.shape
    mod3 = _adaln(c, w_ada, b_ada).reshape(b * 3, 1, d)
    w_main = w_in[:, :4 * D_INNER].astype(BF16)
    wf = jnp.pad(w_in[:, 4 * D_INNER:], ((0, 0), (0, LANES - N_HEADS))).astype(BF16)
    bf = jnp.pad(b_f, (0, LANES - N_HEADS)).reshape(1, LANES)
    q, k, v, z, cum = _inproj(x, mod3, norm_g, w_main, forget=(wf, bf))
    nb = (-cum[:, :, :N_HEADS]).transpose(0, 2, 1).reshape(
        b, N_HEADS // HEADS_PER_BLOCK, HEADS_PER_BLOCK, s)
    nb_spec = pl.BlockSpec((1, 1, HEADS_PER_BLOCK, s), lambda i, j: (i, j, 0, 0))
    u = _mixer(_fox_kernel, q, k, v, z, nb, nb_spec, "fox")
    return _outproj(u, w_out.astype(BF16), x, mod3, final_g)


def _sb_layer(x, c, norm_g, w_ada, b_ada, w_in, w_out, final_g):
    b, s, d = x.shape
    mod3 = _adaln(c, w_ada, b_ada).reshape(b * 3, 1, d)
    q, k, v, z = _inproj(x, mod3, norm_g, w_in.astype(BF16))
    r = lax.broadcasted_iota(jnp.int32, (TK, TK), 0)
    cidx = lax.broadcasted_iota(jnp.int32, (TK, TK), 1)
    tri = (r > cidx).astype(BF16)
    tri_spec = pl.BlockSpec((TK, TK), lambda i, j: (0, 0))
    u = _mixer(_sb_kernel, q, k, v, z, tri, tri_spec, "sb")
    return _outproj(u, w_out.astype(BF16), x, mod3, final_g)


def kernel(x, c, fox_norm_g, fox_w_ada, fox_b_ada, fox_w_in, fox_b_f, fox_w_out,
           sb_norm_g, sb_w_ada, sb_b_ada, sb_w_in, sb_w_out, final_norm_g):
    depth = fox_norm_g.shape[0] + sb_norm_g.shape[0]
    for i in range(depth):
        j = i // 2
        final_g = final_norm_g if i == depth - 1 else None
        if i % 2 == 0:
            x = _fox_layer(x, c, fox_norm_g[j], fox_w_ada[j], fox_b_ada[j],
                           fox_w_in[j], fox_b_f[j], fox_w_out[j], final_g)
        else:
            x = _sb_layer(x, c, sb_norm_g[j], sb_w_ada[j], sb_b_ada[j],
                          sb_w_in[j], sb_w_out[j], final_g)
    return x
```

```python
import math

import jax
import jax.numpy as jnp
from jax import lax
from jax.experimental import pallas as pl
from jax.experimental.pallas import tpu as pltpu

D_MODEL = 1024
D_INNER = 2048
HEAD_DIM = 64
N_HEADS = D_INNER // HEAD_DIM
NORM_EPS = 1e-6
LOG2E = math.log2(math.e)
SM_SCALE = HEAD_DIM ** -0.5
QK_SCALE = SM_SCALE * LOG2E

LANES = 128
HEADS_PER_BLOCK = LANES // HEAD_DIM
VMEM_LIMIT = 56 * 1024 * 1024

ROW_TILE = 512
COL_CHUNK = 512
ADA_COLS = 512
TQ = 256
TK = 256
TQ_ROWS = 256
TK_ROWS = 256
UNROLL = 4
VROWS = HEAD_DIM + 16
BIAS_PARTS = 3
BIAS_STRIDE = 8
MASKED_SCORE = -1e30
TRI_ROWS = TK + 16
FIRST_LIVE_TILE = 3

F32 = jnp.float32
BF16 = jnp.bfloat16


def _split3(x):
    hi = x.astype(BF16)
    r1 = x - hi.astype(F32)
    mid = r1.astype(BF16)
    lo = (r1 - mid.astype(F32)).astype(BF16)
    return hi, mid, lo


def _log_sigmoid(x):
    return jnp.minimum(x, 0.0) - jnp.log1p(jnp.exp(-jnp.abs(x)))


def _adaln_kernel(c_ref, w_ref, b_ref, o_ref):
    c = c_ref[...]
    s = c * jax.nn.sigmoid(c)
    o_ref[...] = jnp.dot(s, w_ref[...], preferred_element_type=F32,
                         precision=lax.Precision.HIGHEST) + b_ref[...]


def _adaln(c, w_ada, b_ada):
    b, d = c.shape
    n = w_ada.shape[1]
    return pl.pallas_call(
        _adaln_kernel,
        grid=(n // ADA_COLS,),
        in_specs=[pl.BlockSpec((b, d), lambda j: (0, 0)),
                  pl.BlockSpec((d, ADA_COLS), lambda j: (0, j)),
                  pl.BlockSpec((1, ADA_COLS), lambda j: (0, j))],
        out_specs=pl.BlockSpec((b, ADA_COLS), lambda j: (0, j)),
        out_shape=jax.ShapeDtypeStruct((b, n), F32),
        name="adaln",
    )(c, w_ada, b_ada.reshape(1, n))


def _modulated_norm(x_ref, shift_ref, scale_ref, g_ref):
    x = x_ref[0]
    y = x * lax.rsqrt(jnp.mean(x * x, axis=-1, keepdims=True) + NORM_EPS) * g_ref[...]
    return (y * (1.0 + scale_ref[0]) + shift_ref[0]).astype(BF16)


def _project(h, wkz_ref, wqvt_ref, qt_ref, k_ref, vt_ref, z_ref):
    chunks = D_INNER // COL_CHUNK
    for o, out_ref in enumerate((k_ref, z_ref)):
        for cc in range(chunks):
            col = o * D_INNER + cc * COL_CHUNK
            out_ref[0, :, cc * COL_CHUNK:(cc + 1) * COL_CHUNK] = jnp.dot(
                h, wkz_ref[:, col:col + COL_CHUNK], preferred_element_type=F32).astype(BF16)
    for o, (out_ref, scale) in enumerate(((qt_ref, QK_SCALE), (vt_ref, None))):
        for cc in range(chunks):
            row = o * D_INNER + cc * COL_CHUNK
            t = lax.dot_general(wqvt_ref[row:row + COL_CHUNK, :], h, (((1,), (1,)), ((), ())),
                                preferred_element_type=F32)
            if scale is not None:
                t = t * scale
            out_ref[0, cc * COL_CHUNK:(cc + 1) * COL_CHUNK, :] = t.astype(BF16)


def _inproj_sb_kernel(x_ref, shift_ref, scale_ref, g_ref, wkz_ref, wqvt_ref,
                      qt_ref, k_ref, vt_ref, z_ref):
    h = _modulated_norm(x_ref, shift_ref, scale_ref, g_ref)
    _project(h, wkz_ref, wqvt_ref, qt_ref, k_ref, vt_ref, z_ref)


def _inproj(x, mod3, g, wkz, wqvt):
    b, s, d = x.shape
    row_spec = lambda width: pl.BlockSpec((1, ROW_TILE, width), lambda i, j: (i, j, 0))
    col_spec = pl.BlockSpec((1, D_INNER, ROW_TILE), lambda i, j: (i, 0, j))
    mod_spec = lambda k: pl.BlockSpec((1, 1, d), lambda i, j: (3 * i + k, 0, 0))
    const = lambda shape: pl.BlockSpec(shape, lambda i, j: (0,) * len(shape),
                                       pipeline_mode=pl.Buffered(1))
    in_specs = [row_spec(d), mod_spec(0), mod_spec(1), const((1, d)),
                const(wkz.shape), const(wqvt.shape)]
    args = [x, mod3, mod3, g.reshape(1, d), wkz, wqvt]
    row_major = jax.ShapeDtypeStruct((b, s, D_INNER), BF16)
    col_major = jax.ShapeDtypeStruct((b, D_INNER, s), BF16)
    out_specs = [col_spec, row_spec(D_INNER), col_spec, row_spec(D_INNER)]
    out_shape = [col_major, row_major, col_major, row_major]
    return pl.pallas_call(
        _inproj_sb_kernel,
        grid=(b, s // ROW_TILE),
        in_specs=in_specs,
        out_specs=out_specs,
        out_shape=out_shape,
        compiler_params=pltpu.CompilerParams(
            dimension_semantics=("parallel", "parallel"), vmem_limit_bytes=VMEM_LIMIT),
        name="inproj_sb",
    )(*args)


def _project_rows(h, w_ref, out_refs):
    for o, out_ref in enumerate(out_refs):
        for cc in range(D_INNER // COL_CHUNK):
            col = o * D_INNER + cc * COL_CHUNK
            out_ref[0, :, cc * COL_CHUNK:(cc + 1) * COL_CHUNK] = jnp.dot(
                h, w_ref[:, col:col + COL_CHUNK], preferred_element_type=F32).astype(BF16)


def _inproj_rows_sb_kernel(x_ref, shift_ref, scale_ref, g_ref, w_ref, q_ref, k_ref, v_ref, z_ref):
    h = _modulated_norm(x_ref, shift_ref, scale_ref, g_ref)
    _project_rows(h, w_ref, (q_ref, k_ref, v_ref, z_ref))


def _inproj_fox_kernel(x_ref, shift_ref, scale_ref, g_ref, w_ref, wf_ref, bf_ref,
                       q_ref, k_ref, v_ref, z_ref, cum_ref, carry_ref):
    h = _modulated_norm(x_ref, shift_ref, scale_ref, g_ref)
    _project_rows(h, w_ref, (q_ref, k_ref, v_ref, z_ref))

    @pl.when(pl.program_id(1) == 0)
    def _():
        carry_ref[...] = jnp.zeros_like(carry_ref)

    log_f = _log_sigmoid(jnp.dot(h, wf_ref[...], preferred_element_type=F32) + bf_ref[...])
    rows = log_f.shape[0]
    r = lax.broadcasted_iota(jnp.int32, (rows, rows), 0)
    c = lax.broadcasted_iota(jnp.int32, (rows, rows), 1)
    tri = (c <= r).astype(BF16)
    hi, mid, lo = _split3(log_f)
    cum = (jnp.dot(tri, hi, preferred_element_type=F32)
           + jnp.dot(tri, mid, preferred_element_type=F32)
           + jnp.dot(tri, lo, preferred_element_type=F32)) + carry_ref[...]
    cum_ref[0] = cum
    carry_ref[...] = cum[rows - 1:rows, :]


def _inproj_rows(x, mod3, g, w, forget=None):
    b, s, d = x.shape
    row_spec = lambda width: pl.BlockSpec((1, ROW_TILE, width), lambda i, j: (i, j, 0))
    mod_spec = lambda k: pl.BlockSpec((1, 1, d), lambda i, j: (3 * i + k, 0, 0))
    const = lambda shape: pl.BlockSpec(shape, lambda i, j: (0,) * len(shape),
                                       pipeline_mode=pl.Buffered(1))
    in_specs = [row_spec(d), mod_spec(0), mod_spec(1), const((1, d)), const(w.shape)]
    args = [x, mod3, mod3, g.reshape(1, d), w]
    out_specs = [row_spec(D_INNER)] * 4
    out_shape = [jax.ShapeDtypeStruct((b, s, D_INNER), BF16)] * 4
    scratch = []
    kernel = _inproj_rows_sb_kernel
    if forget is not None:
        wf, bf = forget
        in_specs += [const(wf.shape), const(bf.shape)]
        args += [wf, bf]
        out_specs.append(row_spec(LANES))
        out_shape.append(jax.ShapeDtypeStruct((b, s, LANES), F32))
        scratch = [pltpu.VMEM((1, LANES), F32)]
        kernel = _inproj_fox_kernel
    return pl.pallas_call(
        kernel,
        grid=(b, s // ROW_TILE),
        in_specs=in_specs,
        out_specs=out_specs,
        out_shape=out_shape,
        scratch_shapes=scratch,
        compiler_params=pltpu.CompilerParams(
            dimension_semantics=("parallel", "arbitrary"), vmem_limit_bytes=VMEM_LIMIT),
        name="inproj_fox",
    )(*args)


def _projection_weights(w_in):
    q, k, v, z = (w_in[:, i * D_INNER:(i + 1) * D_INNER] for i in range(4))
    return (jnp.concatenate([k, z], axis=1).astype(BF16),
            jnp.concatenate([q, v], axis=1).T.astype(BF16))


def _mask_keys_to_head(k, h):
    lane = lax.broadcasted_iota(jnp.int32, k.shape, 1)
    return jnp.where(lane // HEAD_DIM == h, k, jnp.zeros_like(k))


def _finalize(z_ref, o_ref, acc_ref, nq, normalize):
    def body(i, _):
        outs = []
        for h in range(HEADS_PER_BLOCK):
            acc = acc_ref[i, h]
            o = acc[:HEAD_DIM]
            if normalize:
                o = o / acc[HEAD_DIM:HEAD_DIM + 1]
            outs.append(o)
        o = jnp.concatenate(outs, axis=0).T
        z = z_ref[0, pl.ds(i * TQ, TQ), :].astype(F32)
        o_ref[0, pl.ds(i * TQ, TQ), :] = (o * (z * jax.nn.sigmoid(z))).astype(o_ref.dtype)
        return 0

    lax.fori_loop(0, nq, body, 0)


def _wave_pair(n, nq):
    n = jnp.clip(n, 0, nq * (nq + 1) // 2 - 1)
    d = jnp.int32(0)
    for e in range(1, nq):
        d = d + (n >= e * nq - e * (e - 1) // 2).astype(jnp.int32)
    qi = n - (d * nq - d * (d - 1) // 2) + d
    return qi, qi - d


def _sb_kernel(qt_ref, k_ref, vt_ref, z_ref, tri_ref, o_ref,
               kx_ref, mk_ref, s0_ref, s1_ref, lk0_ref, lk1_ref, lb0_ref, lb1_ref,
               p0_ref, p1_ref, r_ref, acc_ref):
    s_len = k_ref.shape[1]
    nq = s_len // TQ
    n_pairs = nq * (nq + 1) // 2
    k = k_ref[0]
    for h in range(HEADS_PER_BLOCK):
        kx_ref[h] = _mask_keys_to_head(k, h)
    key = lax.broadcasted_iota(jnp.int32, (TK, TQ), 0)
    qry = lax.broadcasted_iota(jnp.int32, (TK, TQ), 1)
    mk_ref[0] = jnp.zeros((TK, TQ), F32)
    mk_ref[1] = jnp.where(key < qry, 0.0, MASKED_SCORE)
    r_ref[...] = jnp.zeros_like(r_ref)
    acc_ref[...] = jnp.zeros_like(acc_ref)
    p1_ref[...] = jnp.zeros_like(p1_ref)
    s_bufs, lk_bufs, lb_bufs, p_bufs = ((s0_ref, s1_ref), (lk0_ref, lk1_ref),
                                        (lb0_ref, lb1_ref), (p0_ref, p1_ref))

    pair = lambda n: _wave_pair(n, nq)

    def score_stage(n, s_ref):
        qi, kj = pair(n)
        w = qt_ref[0, :, pl.ds(qi * TQ, TQ)]
        for h in range(HEADS_PER_BLOCK):
            s_ref[h] = jnp.dot(kx_ref[h, pl.ds(kj * TK, TK), :], w, preferred_element_type=F32)

    def gate_stage(n, s_ref, lk_ref, lb_ref):
        qi, kj = pair(n)
        diagonal = (qi == kj).astype(jnp.int32)
        for h in range(HEADS_PER_BLOCK):
            zz = s_ref[h] + mk_ref[diagonal]
            neg_abs = lax.bitcast_convert_type(
                lax.bitcast_convert_type(zz, jnp.uint32) | jnp.uint32(0x80000000), F32)
            lb = jnp.minimum(zz, 0.0) - jnp.log2(1.0 + jnp.exp2(neg_abs))
            lk_ref[h] = (lb - zz).astype(BF16)
            lb_ref[h] = lb

    def weight_stage(lk_ref, lb_ref, p_ref):
        tot = []
        for h in range(HEADS_PER_BLOCK):
            after = jnp.dot(tri_ref[...], lk_ref[h], preferred_element_type=F32)
            p_ref[h] = jnp.exp2(lb_ref[h] + after[:TK]).astype(BF16)
            tot.append(after[TK:TK + 1])
        return tot

    def value_stage(n, p_ref, tot, last):
        qi, kj = pair(n)
        r_out = []
        for h in range(HEADS_PER_BLOCK):
            pv = jnp.dot(vt_ref[0, h * HEAD_DIM:(h + 1) * HEAD_DIM, pl.ds(kj * TK, TK)], p_ref[h],
                         preferred_element_type=F32)
            r = jnp.where(qi == last[0], last[1][h], r_ref[qi, h])
            acc_ref[qi, h] += jnp.exp2(r) * pv
            r_ref[qi, h] = r + tot[h]
            r_out.append(r + tot[h])
        return qi, r_out

    def steps(it, state):
        tot_prv, last = state
        for u in range(UNROLL):
            n = it * UNROLL + u
            score_stage(n + 2, s_bufs[u % 2])
            gate_stage(n + 1, s_bufs[1 - u % 2], lk_bufs[1 - u % 2], lb_bufs[1 - u % 2])
            tot = weight_stage(lk_bufs[u % 2], lb_bufs[u % 2], p_bufs[u % 2])
            last = value_stage(n - 1, p_bufs[1 - u % 2], tot_prv, last)
            tot_prv = tot
        return tot_prv, last

    score_stage(0, s_bufs[0])
    gate_stage(0, s_bufs[0], lk_bufs[0], lb_bufs[0])
    score_stage(1, s_bufs[1])

    n_static = min(-(-(2 * nq - 1) // UNROLL), n_pairs // UNROLL)
    zero_row = [jnp.zeros((1, TQ), F32)] * HEADS_PER_BLOCK
    state = lax.fori_loop(0, n_static, steps, (zero_row, (jnp.int32(-1), zero_row)))

    def live(carry):
        it = carry[0]
        if nq <= FIRST_LIVE_TILE:
            return it < 0
        return (it < n_pairs // UNROLL) & (jnp.max(jnp.exp2(r_ref[FIRST_LIVE_TILE:])) > 0.0)

    def body(carry):
        it, state = carry
        return it + 1, steps(it, state)

    it, (tot_prv, last) = lax.while_loop(live, body, (jnp.int32(n_static), state))
    value_stage(it * UNROLL - 1, p_bufs[1], tot_prv, last)
    _finalize(z_ref, o_ref, acc_ref, nq, False)


def _mixer_call(kernel, name, args, in_specs, scratch, b, s):
    seq_spec = pl.BlockSpec((1, s, LANES), lambda i, j: (i, 0, j))
    return pl.pallas_call(
        kernel,
        grid=(b, D_INNER // LANES),
        in_specs=in_specs,
        out_specs=seq_spec,
        out_shape=jax.ShapeDtypeStruct((b, s, D_INNER), BF16),
        scratch_shapes=scratch,
        compiler_params=pltpu.CompilerParams(
            dimension_semantics=("parallel", "parallel"), vmem_limit_bytes=VMEM_LIMIT),
        name=name,
    )(*args)


def _sb_mixer(qt, k, vt, z):
    b, s, _ = k.shape
    nq = s // TQ
    assert (nq * (nq + 1) // 2) % UNROLL == 0
    r = lax.broadcasted_iota(jnp.int32, (TRI_ROWS, TK), 0)
    c = lax.broadcasted_iota(jnp.int32, (TRI_ROWS, TK), 1)
    tri = ((c > r) | (r >= TK)).astype(BF16)
    seq_spec = pl.BlockSpec((1, s, LANES), lambda i, j: (i, 0, j))
    t_spec = pl.BlockSpec((1, LANES, s), lambda i, j: (i, j, 0))
    tri_spec = pl.BlockSpec((TRI_ROWS, TK), lambda i, j: (0, 0))
    tile = (HEADS_PER_BLOCK, TK, TQ)
    scratch = [pltpu.VMEM((HEADS_PER_BLOCK, s, LANES), BF16),
               pltpu.VMEM((2, TK, TQ), F32),
               pltpu.VMEM(tile, F32), pltpu.VMEM(tile, F32),
               pltpu.VMEM(tile, BF16), pltpu.VMEM(tile, BF16),
               pltpu.VMEM(tile, F32), pltpu.VMEM(tile, F32),
               pltpu.VMEM(tile, BF16), pltpu.VMEM(tile, BF16),
               pltpu.VMEM((nq, HEADS_PER_BLOCK, 1, TQ), F32),
               pltpu.VMEM((nq, HEADS_PER_BLOCK, HEAD_DIM, TQ), F32)]
    return _mixer_call(_sb_kernel, "sb", (qt, k, vt, z, tri),
                       [t_spec, seq_spec, t_spec, seq_spec, tri_spec], scratch, b, s)


def _head_masked_queries(q_ref, i):
    q = q_ref[0, pl.ds(i * TQ_ROWS, TQ_ROWS), :] * SM_SCALE
    lane = lax.broadcasted_iota(jnp.int32, q.shape, 1)
    zero = jnp.zeros_like(q)
    return jnp.concatenate([jnp.where(lane < HEAD_DIM, q, zero),
                            jnp.where(lane >= HEAD_DIM, q, zero)], axis=0)


def _scores_rows(qq, k_ref, j):
    kt = k_ref[0, pl.ds(j * TK_ROWS, TK_ROWS), :]
    return lax.dot_general(qq, kt, (((1,), (1,)), ((), ())), preferred_element_type=F32)


def _gated_store(o_ref, z_ref, i, outs):
    lane = lax.broadcasted_iota(jnp.int32, outs[0].shape, 1)
    o = jnp.where(lane < HEAD_DIM, outs[0], outs[1])
    z = z_ref[0, pl.ds(i * TQ_ROWS, TQ_ROWS), :].astype(F32)
    o_ref[0, pl.ds(i * TQ_ROWS, TQ_ROWS), :] = (o * (z * jax.nn.sigmoid(z))).astype(o_ref.dtype)


def _fox_rows_kernel(q_ref, k_ref, v_ref, nb_ref, z_ref, o_ref):
    def q_tile(i, _):
        qq = _head_masked_queries(q_ref, i)
        row = lax.broadcasted_iota(jnp.int32, (TQ_ROWS, TK_ROWS), 0)
        col = lax.broadcasted_iota(jnp.int32, (TQ_ROWS, TK_ROWS), 1)

        s = _scores_rows(qq, k_ref, i)
        vt = v_ref[0, pl.ds(i * TK_ROWS, TK_ROWS), :]
        state = []
        for h in range(HEADS_PER_BLOCK):
            sh = s[h * TQ_ROWS:(h + 1) * TQ_ROWS] + nb_ref[0, 0, h:h + 1, pl.ds(i * TK_ROWS, TK_ROWS)]
            sh = jnp.where(col <= row, sh, -jnp.inf)
            m = jnp.max(sh, axis=-1, keepdims=True)
            p = jnp.exp(sh - m)
            l = jnp.sum(p, axis=-1, keepdims=True)
            acc = jnp.dot(p.astype(BF16), vt, preferred_element_type=F32)
            state += [m, l, acc]

        def kv_tile(j, state):
            s = _scores_rows(qq, k_ref, j)
            vt = v_ref[0, pl.ds(j * TK_ROWS, TK_ROWS), :]
            new = []
            for h in range(HEADS_PER_BLOCK):
                m, l, acc = state[3 * h:3 * h + 3]
                sh = s[h * TQ_ROWS:(h + 1) * TQ_ROWS] + nb_ref[0, 0, h:h + 1, pl.ds(j * TK_ROWS, TK_ROWS)]
                m_new = jnp.maximum(m, jnp.max(sh, axis=-1, keepdims=True))
                alpha = jnp.exp(m - m_new)
                p = jnp.exp(sh - m_new)
                l = alpha * l + jnp.sum(p, axis=-1, keepdims=True)
                acc = alpha * acc + jnp.dot(p.astype(BF16), vt, preferred_element_type=F32)
                new += [m_new, l, acc]
            return tuple(new)

        state = lax.fori_loop(0, i, kv_tile, tuple(state))
        outs = [state[3 * h + 2] / state[3 * h + 1] for h in range(HEADS_PER_BLOCK)]
        _gated_store(o_ref, z_ref, i, outs)
        return 0

    lax.fori_loop(0, q_ref.shape[1] // TQ_ROWS, q_tile, 0)


def _mixer(kernel, q, k, v, z, extra, extra_spec, name):
    b, s, _ = q.shape
    seq_spec = pl.BlockSpec((1, s, LANES), lambda i, j: (i, 0, j))
    return pl.pallas_call(
        kernel,
        grid=(b, D_INNER // LANES),
        in_specs=[seq_spec, seq_spec, seq_spec] + (
            [extra_spec, seq_spec] if name == "fox" else [seq_spec, extra_spec]),
        out_specs=seq_spec,
        out_shape=jax.ShapeDtypeStruct((b, s, D_INNER), BF16),
        compiler_params=pltpu.CompilerParams(
            dimension_semantics=("parallel", "parallel"), vmem_limit_bytes=VMEM_LIMIT),
        name=name,
    )(*((q, k, v, extra, z) if name == "fox" else (q, k, v, z, extra)))


def _outproj_kernel(u_ref, w_ref, x_ref, gate_ref, o_ref):
    y = jnp.dot(u_ref[0], w_ref[...], preferred_element_type=F32)
    o_ref[0] = x_ref[0] + gate_ref[0] * y


def _outproj_final_kernel(u_ref, w_ref, x_ref, gate_ref, g_ref, o_ref):
    y = jnp.dot(u_ref[0], w_ref[...], preferred_element_type=F32)
    r = x_ref[0] + gate_ref[0] * y
    o_ref[0] = r * lax.rsqrt(jnp.mean(r * r, axis=-1, keepdims=True) + NORM_EPS) * g_ref[...]


def _outproj(u, w, x, mod3, final_g=None):
    b, s, d = x.shape
    row_spec = lambda width: pl.BlockSpec((1, ROW_TILE, width), lambda i, j: (i, j, 0))
    const = lambda shape: pl.BlockSpec(shape, lambda i, j: (0,) * len(shape),
                                       pipeline_mode=pl.Buffered(1))
    in_specs = [row_spec(D_INNER), const(w.shape), row_spec(d),
                pl.BlockSpec((1, 1, d), lambda i, j: (3 * i + 2, 0, 0))]
    args = [u, w, x, mod3]
    kernel = _outproj_kernel
    if final_g is not None:
        in_specs.append(const((1, d)))
        args.append(final_g.reshape(1, d))
        kernel = _outproj_final_kernel
    return pl.pallas_call(
        kernel,
        grid=(b, s // ROW_TILE),
        in_specs=in_specs,
        out_specs=row_spec(d),
        out_shape=jax.ShapeDtypeStruct((b, s, d), F32),
        compiler_params=pltpu.CompilerParams(
            dimension_semantics=("parallel", "parallel"), vmem_limit_bytes=VMEM_LIMIT),
        name="outproj_final" if final_g is not None else "outproj",
    )(*args)


def _fox_layer(x, c, norm_g, w_ada, b_ada, w_in, b_f, w_out, final_g):
    b, s, d = x.shape
    mod3 = _adaln(c, w_ada, b_ada).reshape(b * 3, 1, d)
    w_main = w_in[:, :4 * D_INNER].astype(BF16)
    wf = jnp.pad(w_in[:, 4 * D_INNER:], ((0, 0), (0, LANES - N_HEADS))).astype(BF16)
    bf = jnp.pad(b_f, (0, LANES - N_HEADS)).reshape(1, LANES)
    q, k, v, z, cum = _inproj_rows(x, mod3, norm_g, w_main, forget=(wf, bf))
    nb = (-cum[:, :, :N_HEADS]).transpose(0, 2, 1).reshape(
        b, N_HEADS // HEADS_PER_BLOCK, HEADS_PER_BLOCK, s)
    nb_spec = pl.BlockSpec((1, 1, HEADS_PER_BLOCK, s), lambda i, j: (i, j, 0, 0))
    u = _mixer(_fox_rows_kernel, q, k, v, z, nb, nb_spec, "fox")
    return _outproj(u, w_out.astype(BF16), x, mod3, final_g)


def _sb_layer(x, c, norm_g, w_ada, b_ada, w_in, w_out, final_g):
    b, s, d = x.shape
    mod3 = _adaln(c, w_ada, b_ada).reshape(b * 3, 1, d)
    wkz, wqvt = _projection_weights(w_in)
    qt, k, vt, z = _inproj(x, mod3, norm_g, wkz, wqvt)
    u = _sb_mixer(qt, k, vt, z)
    return _outproj(u, w_out.astype(BF16), x, mod3, final_g)


def kernel(x, c, fox_norm_g, fox_w_ada, fox_b_ada, fox_w_in, fox_b_f, fox_w_out,
           sb_norm_g, sb_w_ada, sb_b_ada, sb_w_in, sb_w_out, final_norm_g):
    depth = fox_norm_g.shape[0] + sb_norm_g.shape[0]
    for i in range(depth):
        j = i // 2
        final_g = final_norm_g if i == depth - 1 else None
        if i % 2 == 0:
            x = _fox_layer(x, c, fox_norm_g[j], fox_w_ada[j], fox_b_ada[j],
                           fox_w_in[j], fox_b_f[j], fox_w_out[j], final_g)
        else:
            x = _sb_layer(x, c, sb_norm_g[j], sb_w_ada[j], sb_b_ada[j],
                          sb_w_in[j], sb_w_out[j], final_g)
    return x
```

```python
import math

import jax
import jax.numpy as jnp
from jax import lax
from jax.experimental import pallas as pl
from jax.experimental.pallas import tpu as pltpu

D_MODEL = 1024
D_INNER = 2048
HEAD_DIM = 64
N_HEADS = D_INNER // HEAD_DIM
NORM_EPS = 1e-6
LOG2E = math.log2(math.e)
QK_SCALE = HEAD_DIM ** -0.5 * LOG2E

LANES = 128
HEADS_PER_BLOCK = LANES // HEAD_DIM
VMEM_LIMIT = 56 * 1024 * 1024

ROW_TILE = 512
COL_CHUNK = 512
ADA_COLS = 512
TQ = 256
TK = 256
UNROLL = 4
VROWS = 2 * HEAD_DIM
BIAS_PARTS = 3
BIAS_STRIDE = 8
MASKED_SCORE = -1e30
TRI_ROWS = TK + 16
FIRST_LIVE_TILE = 3

F32 = jnp.float32
BF16 = jnp.bfloat16


def _split3(x):
    hi = x.astype(BF16)
    r1 = x - hi.astype(F32)
    mid = r1.astype(BF16)
    lo = (r1 - mid.astype(F32)).astype(BF16)
    return hi, mid, lo


def _log_sigmoid(x):
    return jnp.minimum(x, 0.0) - jnp.log1p(jnp.exp(-jnp.abs(x)))


def _adaln_kernel(c_ref, w_ref, b_ref, o_ref):
    c = c_ref[...]
    s = c * jax.nn.sigmoid(c)
    o_ref[...] = jnp.dot(s, w_ref[...], preferred_element_type=F32,
                         precision=lax.Precision.HIGHEST) + b_ref[...]


def _adaln(c, w_ada, b_ada):
    b, d = c.shape
    n = w_ada.shape[1]
    return pl.pallas_call(
        _adaln_kernel,
        grid=(n // ADA_COLS,),
        in_specs=[pl.BlockSpec((b, d), lambda j: (0, 0)),
                  pl.BlockSpec((d, ADA_COLS), lambda j: (0, j)),
                  pl.BlockSpec((1, ADA_COLS), lambda j: (0, j))],
        out_specs=pl.BlockSpec((b, ADA_COLS), lambda j: (0, j)),
        out_shape=jax.ShapeDtypeStruct((b, n), F32),
        name="adaln",
    )(c, w_ada, b_ada.reshape(1, n))


def _modulated_norm(x_ref, shift_ref, scale_ref, g_ref):
    x = x_ref[0]
    y = x * lax.rsqrt(jnp.mean(x * x, axis=-1, keepdims=True) + NORM_EPS) * g_ref[...]
    return (y * (1.0 + scale_ref[0]) + shift_ref[0]).astype(BF16)


def _project(h, wkz_ref, wqvt_ref, qt_ref, k_ref, vt_ref, z_ref):
    chunks = D_INNER // COL_CHUNK
    for o, out_ref in enumerate((k_ref, z_ref)):
        for cc in range(chunks):
            col = o * D_INNER + cc * COL_CHUNK
            out_ref[0, :, cc * COL_CHUNK:(cc + 1) * COL_CHUNK] = jnp.dot(
                h, wkz_ref[:, col:col + COL_CHUNK], preferred_element_type=F32).astype(BF16)
    for o, (out_ref, scale) in enumerate(((qt_ref, QK_SCALE), (vt_ref, None))):
        for cc in range(chunks):
            row = o * D_INNER + cc * COL_CHUNK
            t = lax.dot_general(wqvt_ref[row:row + COL_CHUNK, :], h, (((1,), (1,)), ((), ())),
                                preferred_element_type=F32)
            if scale is not None:
                t = t * scale
            out_ref[0, cc * COL_CHUNK:(cc + 1) * COL_CHUNK, :] = t.astype(BF16)


def _inproj_sb_kernel(x_ref, shift_ref, scale_ref, g_ref, wkz_ref, wqvt_ref,
                      qt_ref, k_ref, vt_ref, z_ref):
    h = _modulated_norm(x_ref, shift_ref, scale_ref, g_ref)
    _project(h, wkz_ref, wqvt_ref, qt_ref, k_ref, vt_ref, z_ref)


def _inproj_fox_kernel(x_ref, shift_ref, scale_ref, g_ref, wkz_ref, wqvt_ref, wf_ref, bf_ref,
                       qt_ref, k_ref, vt_ref, z_ref, hi_ref, mid_ref, lo_ref, carry_ref):
    h = _modulated_norm(x_ref, shift_ref, scale_ref, g_ref)
    _project(h, wkz_ref, wqvt_ref, qt_ref, k_ref, vt_ref, z_ref)

    @pl.when(pl.program_id(1) == 0)
    def _():
        carry_ref[...] = jnp.zeros_like(carry_ref)

    log_f = _log_sigmoid(jnp.dot(h, wf_ref[...], preferred_element_type=F32) + bf_ref[...])
    rows = log_f.shape[0]
    r = lax.broadcasted_iota(jnp.int32, (rows, rows), 0)
    c = lax.broadcasted_iota(jnp.int32, (rows, rows), 1)
    tri = (c <= r).astype(BF16)
    hi, mid, lo = _split3(log_f)
    cum = (jnp.dot(tri, hi, preferred_element_type=F32)
           + jnp.dot(tri, mid, preferred_element_type=F32)
           + jnp.dot(tri, lo, preferred_element_type=F32)) + carry_ref[...]
    carry_ref[...] = cum[rows - 1:rows, :]
    hi_ref[0], mid_ref[0], lo_ref[0] = _split3(cum * (-LOG2E))


def _inproj(x, mod3, g, wkz, wqvt, forget=None):
    b, s, d = x.shape
    row_spec = lambda width: pl.BlockSpec((1, ROW_TILE, width), lambda i, j: (i, j, 0))
    col_spec = pl.BlockSpec((1, D_INNER, ROW_TILE), lambda i, j: (i, 0, j))
    mod_spec = lambda k: pl.BlockSpec((1, 1, d), lambda i, j: (3 * i + k, 0, 0))
    const = lambda shape: pl.BlockSpec(shape, lambda i, j: (0,) * len(shape),
                                       pipeline_mode=pl.Buffered(1))
    in_specs = [row_spec(d), mod_spec(0), mod_spec(1), const((1, d)),
                const(wkz.shape), const(wqvt.shape)]
    args = [x, mod3, mod3, g.reshape(1, d), wkz, wqvt]
    row_major = jax.ShapeDtypeStruct((b, s, D_INNER), BF16)
    col_major = jax.ShapeDtypeStruct((b, D_INNER, s), BF16)
    out_specs = [col_spec, row_spec(D_INNER), col_spec, row_spec(D_INNER)]
    out_shape = [col_major, row_major, col_major, row_major]
    scratch = []
    kernel = _inproj_sb_kernel
    if forget is not None:
        wf, bf = forget
        in_specs += [const(wf.shape), const(bf.shape)]
        args += [wf, bf]
        out_specs += [row_spec(LANES)] * BIAS_PARTS
        out_shape += [jax.ShapeDtypeStruct((b, s, LANES), BF16)] * BIAS_PARTS
        scratch = [pltpu.VMEM((1, LANES), F32)]
        kernel = _inproj_fox_kernel
    return pl.pallas_call(
        kernel,
        grid=(b, s // ROW_TILE),
        in_specs=in_specs,
        out_specs=out_specs,
        out_shape=out_shape,
        scratch_shapes=scratch,
        compiler_params=pltpu.CompilerParams(
            dimension_semantics=("parallel", "arbitrary"), vmem_limit_bytes=VMEM_LIMIT),
        name="inproj_fox" if forget is not None else "inproj_sb",
    )(*args)


def _projection_weights(w_in):
    q, k, v, z = (w_in[:, i * D_INNER:(i + 1) * D_INNER] for i in range(4))
    return (jnp.concatenate([k, z], axis=1).astype(BF16),
            jnp.concatenate([q, v], axis=1).T.astype(BF16))


def _mask_keys_to_head(k, h):
    lane = lax.broadcasted_iota(jnp.int32, k.shape, 1)
    return jnp.where(lane // HEAD_DIM == h, k, jnp.zeros_like(k))


def _finalize(z_ref, o_ref, acc_ref, nq, normalize):
    def body(i, _):
        outs = []
        for h in range(HEADS_PER_BLOCK):
            acc = acc_ref[i, h]
            o = acc[:HEAD_DIM]
            if normalize:
                o = o / acc[HEAD_DIM:HEAD_DIM + 1]
            outs.append(o)
        o = jnp.concatenate(outs, axis=0).T
        z = z_ref[0, pl.ds(i * TQ, TQ), :].astype(F32)
        o_ref[0, pl.ds(i * TQ, TQ), :] = (o * (z * jax.nn.sigmoid(z))).astype(o_ref.dtype)
        return 0

    lax.fori_loop(0, nq, body, 0)


def _wave_pair(n, nq):
    n = jnp.clip(n, 0, nq * (nq + 1) // 2 - 1)
    d = jnp.int32(0)
    for e in range(1, nq):
        d = d + (n >= e * nq - e * (e - 1) // 2).astype(jnp.int32)
    qi = n - (d * nq - d * (d - 1) // 2) + d
    return qi, qi - d


def _fox_kernel(qt_ref, k_ref, bp_ref, vt_ref, z_ref, o_ref,
                kx_ref, w_ref, vx_ref, mk_ref, s0_ref, s1_ref, p0_ref, p1_ref, m_ref, acc_ref):
    s_len = k_ref.shape[1]
    nq = s_len // TQ
    n_pairs = nq * (nq + 1) // 2
    hp = pl.program_id(1)
    k = k_ref[0]
    bp = bp_ref[0]
    lane = lax.broadcasted_iota(jnp.int32, bp.shape, 1)
    for h in range(HEADS_PER_BLOCK):
        base = BIAS_STRIDE * hp + BIAS_PARTS * h
        kx_ref[h, :, :LANES] = _mask_keys_to_head(k, h)
        kx_ref[h, :, LANES:] = jnp.where((lane >= base) & (lane < base + BIAS_PARTS),
                                         bp, jnp.zeros_like(bp))
        vx_ref[h, :HEAD_DIM, :] = vt_ref[0, h * HEAD_DIM:(h + 1) * HEAD_DIM, :]
        vx_ref[h, HEAD_DIM:, :] = jnp.ones((VROWS - HEAD_DIM, s_len), BF16)
    w_ref[:LANES, :] = qt_ref[0]
    w_ref[LANES:, :] = jnp.ones((LANES, s_len), BF16)
    key = lax.broadcasted_iota(jnp.int32, (TK, TQ), 0)
    qry = lax.broadcasted_iota(jnp.int32, (TK, TQ), 1)
    mk_ref[0] = jnp.zeros((TK, TQ), F32)
    mk_ref[1] = jnp.where(key <= qry, 0.0, MASKED_SCORE)
    m_ref[...] = jnp.full(m_ref.shape, MASKED_SCORE, F32)
    acc_ref[...] = jnp.zeros_like(acc_ref)
    p1_ref[...] = jnp.zeros_like(p1_ref)
    s_bufs, p_bufs = (s0_ref, s1_ref), (p0_ref, p1_ref)

    def score_stage(n, s_ref):
        qi, kj = _wave_pair(n, nq)
        w = w_ref[:, pl.ds(qi * TQ, TQ)]
        for h in range(HEADS_PER_BLOCK):
            s_ref[h] = jnp.dot(kx_ref[h, pl.ds(kj * TK, TK), :], w, preferred_element_type=F32)

    def softmax_stage(n, s_ref, p_ref, last):
        qi, kj = _wave_pair(n, nq)
        diagonal = (qi == kj).astype(jnp.int32)
        m_out, al_out = [], []
        for h in range(HEADS_PER_BLOCK):
            st = s_ref[h] + mk_ref[diagonal]
            m = jnp.where(qi == last[0], last[1][h], m_ref[qi, h])
            m_new = jnp.maximum(m, jnp.max(st, axis=0, keepdims=True))
            al_out.append(jnp.exp2(m - m_new))
            p_ref[h] = jnp.exp2(st - m_new).astype(BF16)
            m_ref[qi, h] = m_new
            m_out.append(m_new)
        return (qi, m_out), al_out

    def value_stage(n, p_ref, al):
        qi, kj = _wave_pair(n, nq)
        for h in range(HEADS_PER_BLOCK):
            pv = jnp.dot(vx_ref[h, :, pl.ds(kj * TK, TK)], p_ref[h], preferred_element_type=F32)
            acc_ref[qi, h] = al[h] * acc_ref[qi, h] + pv

    def steps(it, state):
        al_prv, last = state
        for u in range(UNROLL):
            n = it * UNROLL + u
            score_stage(n + 1, s_bufs[1 - u % 2])
            last, al = softmax_stage(n, s_bufs[u % 2], p_bufs[u % 2], last)
            value_stage(n - 1, p_bufs[1 - u % 2], al_prv)
            al_prv = al
        return al_prv, last

    score_stage(0, s_bufs[0])
    row = lambda value: [jnp.full((1, TQ), value, F32)] * HEADS_PER_BLOCK
    al_prv, _ = lax.fori_loop(0, n_pairs // UNROLL, steps, (row(1.0), (jnp.int32(-1), row(0.0))))
    value_stage(n_pairs - 1, p_bufs[1], al_prv)
    _finalize(z_ref, o_ref, acc_ref, nq, True)


def _sb_kernel(qt_ref, k_ref, vt_ref, z_ref, tri_ref, o_ref,
               kx_ref, mk_ref, s0_ref, s1_ref, lk0_ref, lk1_ref, lb0_ref, lb1_ref,
               p0_ref, p1_ref, r_ref, acc_ref):
    s_len = k_ref.shape[1]
    nq = s_len // TQ
    n_pairs = nq * (nq + 1) // 2
    k = k_ref[0]
    for h in range(HEADS_PER_BLOCK):
        kx_ref[h] = _mask_keys_to_head(k, h)
    key = lax.broadcasted_iota(jnp.int32, (TK, TQ), 0)
    qry = lax.broadcasted_iota(jnp.int32, (TK, TQ), 1)
    mk_ref[0] = jnp.zeros((TK, TQ), F32)
    mk_ref[1] = jnp.where(key < qry, 0.0, MASKED_SCORE)
    r_ref[...] = jnp.zeros_like(r_ref)
    acc_ref[...] = jnp.zeros_like(acc_ref)
    p1_ref[...] = jnp.zeros_like(p1_ref)
    s_bufs, lk_bufs, lb_bufs, p_bufs = ((s0_ref, s1_ref), (lk0_ref, lk1_ref),
                                        (lb0_ref, lb1_ref), (p0_ref, p1_ref))

    pair = lambda n: _wave_pair(n, nq)

    def score_stage(n, s_ref):
        qi, kj = pair(n)
        w = qt_ref[0, :, pl.ds(qi * TQ, TQ)]
        for h in range(HEADS_PER_BLOCK):
            s_ref[h] = jnp.dot(kx_ref[h, pl.ds(kj * TK, TK), :], w, preferred_element_type=F32)

    def gate_stage(n, s_ref, lk_ref, lb_ref):
        qi, kj = pair(n)
        diagonal = (qi == kj).astype(jnp.int32)
        for h in range(HEADS_PER_BLOCK):
            zz = s_ref[h] + mk_ref[diagonal]
            neg_abs = lax.bitcast_convert_type(
                lax.bitcast_convert_type(zz, jnp.uint32) | jnp.uint32(0x80000000), F32)
            lb = jnp.minimum(zz, 0.0) - jnp.log2(1.0 + jnp.exp2(neg_abs))
            lk_ref[h] = (lb - zz).astype(BF16)
            lb_ref[h] = lb

    def weight_stage(lk_ref, lb_ref, p_ref):
        tot = []
        for h in range(HEADS_PER_BLOCK):
            after = jnp.dot(tri_ref[...], lk_ref[h], preferred_element_type=F32)
            p_ref[h] = jnp.exp2(lb_ref[h] + after[:TK]).astype(BF16)
            tot.append(after[TK:TK + 1])
        return tot

    def value_stage(n, p_ref, tot, last):
        qi, kj = pair(n)
        r_out = []
        for h in range(HEADS_PER_BLOCK):
            pv = jnp.dot(vt_ref[0, h * HEAD_DIM:(h + 1) * HEAD_DIM, pl.ds(kj * TK, TK)], p_ref[h],
                         preferred_element_type=F32)
            r = jnp.where(qi == last[0], last[1][h], r_ref[qi, h])
            acc_ref[qi, h] += jnp.exp2(r) * pv
            r_ref[qi, h] = r + tot[h]
            r_out.append(r + tot[h])
        return qi, r_out

    def steps(it, state):
        tot_prv, last = state
        for u in range(UNROLL):
            n = it * UNROLL + u
            score_stage(n + 2, s_bufs[u % 2])
            gate_stage(n + 1, s_bufs[1 - u % 2], lk_bufs[1 - u % 2], lb_bufs[1 - u % 2])
            tot = weight_stage(lk_bufs[u % 2], lb_bufs[u % 2], p_bufs[u % 2])
            last = value_stage(n - 1, p_bufs[1 - u % 2], tot_prv, last)
            tot_prv = tot
        return tot_prv, last

    score_stage(0, s_bufs[0])
    gate_stage(0, s_bufs[0], lk_bufs[0], lb_bufs[0])
    score_stage(1, s_bufs[1])

    n_static = min(-(-(2 * nq - 1) // UNROLL), n_pairs // UNROLL)
    zero_row = [jnp.zeros((1, TQ), F32)] * HEADS_PER_BLOCK
    state = lax.fori_loop(0, n_static, steps, (zero_row, (jnp.int32(-1), zero_row)))

    def live(carry):
        it = carry[0]
        if nq <= FIRST_LIVE_TILE:
            return it < 0
        return (it < n_pairs // UNROLL) & (jnp.max(jnp.exp2(r_ref[FIRST_LIVE_TILE:])) > 0.0)

    def body(carry):
        it, state = carry
        return it + 1, steps(it, state)

    it, (tot_prv, last) = lax.while_loop(live, body, (jnp.int32(n_static), state))
    value_stage(it * UNROLL - 1, p_bufs[1], tot_prv, last)
    _finalize(z_ref, o_ref, acc_ref, nq, False)


def _mixer_call(kernel, name, args, in_specs, scratch, b, s):
    seq_spec = pl.BlockSpec((1, s, LANES), lambda i, j: (i, 0, j))
    return pl.pallas_call(
        kernel,
        grid=(b, D_INNER // LANES),
        in_specs=in_specs,
        out_specs=seq_spec,
        out_shape=jax.ShapeDtypeStruct((b, s, D_INNER), BF16),
        scratch_shapes=scratch,
        compiler_params=pltpu.CompilerParams(
            dimension_semantics=("parallel", "parallel"), vmem_limit_bytes=VMEM_LIMIT),
        name=name,
    )(*args)


def _fox_mixer(qt, k, bp, vt, z):
    b, s, _ = k.shape
    nq = s // TQ
    assert (nq * (nq + 1) // 2) % UNROLL == 0
    seq_spec = pl.BlockSpec((1, s, LANES), lambda i, j: (i, 0, j))
    t_spec = pl.BlockSpec((1, LANES, s), lambda i, j: (i, j, 0))
    bp_spec = pl.BlockSpec((1, s, LANES), lambda i, j: (i, 0, 0))
    tile = (HEADS_PER_BLOCK, TK, TQ)
    scratch = [pltpu.VMEM((HEADS_PER_BLOCK, s, 2 * LANES), BF16),
               pltpu.VMEM((2 * LANES, s), BF16),
               pltpu.VMEM((HEADS_PER_BLOCK, VROWS, s), BF16),
               pltpu.VMEM((2, TK, TQ), F32),
               pltpu.VMEM(tile, F32), pltpu.VMEM(tile, F32),
               pltpu.VMEM(tile, BF16), pltpu.VMEM(tile, BF16),
               pltpu.VMEM((nq, HEADS_PER_BLOCK, 1, TQ), F32),
               pltpu.VMEM((nq, HEADS_PER_BLOCK, VROWS, TQ), F32)]
    return _mixer_call(_fox_kernel, "fox", (qt, k, bp, vt, z),
                       [t_spec, seq_spec, bp_spec, t_spec, seq_spec], scratch, b, s)


def _sb_mixer(qt, k, vt, z):
    b, s, _ = k.shape
    nq = s // TQ
    assert (nq * (nq + 1) // 2) % UNROLL == 0
    r = lax.broadcasted_iota(jnp.int32, (TRI_ROWS, TK), 0)
    c = lax.broadcasted_iota(jnp.int32, (TRI_ROWS, TK), 1)
    tri = ((c > r) | (r >= TK)).astype(BF16)
    seq_spec = pl.BlockSpec((1, s, LANES), lambda i, j: (i, 0, j))
    t_spec = pl.BlockSpec((1, LANES, s), lambda i, j: (i, j, 0))
    tri_spec = pl.BlockSpec((TRI_ROWS, TK), lambda i, j: (0, 0))
    tile = (HEADS_PER_BLOCK, TK, TQ)
    scratch = [pltpu.VMEM((HEADS_PER_BLOCK, s, LANES), BF16),
               pltpu.VMEM((2, TK, TQ), F32),
               pltpu.VMEM(tile, F32), pltpu.VMEM(tile, F32),
               pltpu.VMEM(tile, BF16), pltpu.VMEM(tile, BF16),
               pltpu.VMEM(tile, F32), pltpu.VMEM(tile, F32),
               pltpu.VMEM(tile, BF16), pltpu.VMEM(tile, BF16),
               pltpu.VMEM((nq, HEADS_PER_BLOCK, 1, TQ), F32),
               pltpu.VMEM((nq, HEADS_PER_BLOCK, HEAD_DIM, TQ), F32)]
    return _mixer_call(_sb_kernel, "sb", (qt, k, vt, z, tri),
                       [t_spec, seq_spec, t_spec, seq_spec, tri_spec], scratch, b, s)


def _outproj_kernel(u_ref, w_ref, x_ref, gate_ref, o_ref):
    y = jnp.dot(u_ref[0], w_ref[...], preferred_element_type=F32)
    o_ref[0] = x_ref[0] + gate_ref[0] * y


def _outproj_final_kernel(u_ref, w_ref, x_ref, gate_ref, g_ref, o_ref):
    y = jnp.dot(u_ref[0], w_ref[...], preferred_element_type=F32)
    r = x_ref[0] + gate_ref[0] * y
    o_ref[0] = r * lax.rsqrt(jnp.mean(r * r, axis=-1, keepdims=True) + NORM_EPS) * g_ref[...]


def _outproj(u, w, x, mod3, final_g=None):
    b, s, d = x.shape
    row_spec = lambda width: pl.BlockSpec((1, ROW_TILE, width), lambda i, j: (i, j, 0))
    const = lambda shape: pl.BlockSpec(shape, lambda i, j: (0,) * len(shape),
                                       pipeline_mode=pl.Buffered(1))
    in_specs = [row_spec(D_INNER), const(w.shape), row_spec(d),
                pl.BlockSpec((1, 1, d), lambda i, j: (3 * i + 2, 0, 0))]
    args = [u, w, x, mod3]
    kernel = _outproj_kernel
    if final_g is not None:
        in_specs.append(const((1, d)))
        args.append(final_g.reshape(1, d))
        kernel = _outproj_final_kernel
    return pl.pallas_call(
        kernel,
        grid=(b, s // ROW_TILE),
        in_specs=in_specs,
        out_specs=row_spec(d),
        out_shape=jax.ShapeDtypeStruct((b, s, d), F32),
        compiler_params=pltpu.CompilerParams(
            dimension_semantics=("parallel", "parallel"), vmem_limit_bytes=VMEM_LIMIT),
        name="outproj_final" if final_g is not None else "outproj",
    )(*args)


def _arrange_bias_parts(parts):
    b, s, _ = parts[0].shape
    x = jnp.stack([p[:, :, :N_HEADS] for p in parts], axis=-1)
    x = x.reshape(b, s, N_HEADS // HEADS_PER_BLOCK, HEADS_PER_BLOCK * BIAS_PARTS)
    x = jnp.pad(x, ((0, 0), (0, 0), (0, 0), (0, BIAS_STRIDE - HEADS_PER_BLOCK * BIAS_PARTS)))
    return x.reshape(b, s, LANES)


def _fox_layer(x, c, norm_g, w_ada, b_ada, w_in, b_f, w_out, final_g):
    b, s, d = x.shape
    mod3 = _adaln(c, w_ada, b_ada).reshape(b * 3, 1, d)
    wkz, wqvt = _projection_weights(w_in)
    wf = jnp.pad(w_in[:, 4 * D_INNER:], ((0, 0), (0, LANES - N_HEADS))).astype(BF16)
    bf = jnp.pad(b_f, (0, LANES - N_HEADS)).reshape(1, LANES)
    qt, k, vt, z, hi, mid, lo = _inproj(x, mod3, norm_g, wkz, wqvt, forget=(wf, bf))
    u = _fox_mixer(qt, k, _arrange_bias_parts((hi, mid, lo)), vt, z)
    return _outproj(u, w_out.astype(BF16), x, mod3, final_g)


def _sb_layer(x, c, norm_g, w_ada, b_ada, w_in, w_out, final_g):
    b, s, d = x.shape
    mod3 = _adaln(c, w_ada, b_ada).reshape(b * 3, 1, d)
    wkz, wqvt = _projection_weights(w_in)
    qt, k, vt, z = _inproj(x, mod3, norm_g, wkz, wqvt)
    u = _sb_mixer(qt, k, vt, z)
    return _outproj(u, w_out.astype(BF16), x, mod3, final_g)


def kernel(x, c, fox_norm_g, fox_w_ada, fox_b_ada, fox_w_in, fox_b_f, fox_w_out,
           sb_norm_g, sb_w_ada, sb_b_ada, sb_w_in, sb_w_out, final_norm_g):
    depth = fox_norm_g.shape[0] + sb_norm_g.shape[0]
    for i in range(depth):
        j = i // 2
        final_g = final_norm_g if i == depth - 1 else None
        if i % 2 == 0:
            x = _fox_layer(x, c, fox_norm_g[j], fox_w_ada[j], fox_b_ada[j],
                           fox_w_in[j], fox_b_f[j], fox_w_out[j], final_g)
        else:
            x = _sb_layer(x, c, sb_norm_g[j], sb_w_ada[j], sb_b_ada[j],
                          sb_w_in[j], sb_w_out[j], final_g)
    return x
```

```python
import math

import jax
import jax.numpy as jnp
from jax import lax
from jax.experimental import pallas as pl
from jax.experimental.pallas import tpu as pltpu

D_MODEL = 1024
D_INNER = 2048
HEAD_DIM = 64
N_HEADS = D_INNER // HEAD_DIM
NORM_EPS = 1e-6
LOG2E = math.log2(math.e)
QK_SCALE = HEAD_DIM ** -0.5 * LOG2E

LANES = 128
HEADS_PER_BLOCK = LANES // HEAD_DIM
VMEM_LIMIT = 56 * 1024 * 1024

ROW_TILE = 512
COL_CHUNK = 512
ADA_COLS = 512
TQ = 256
TK = 256
UNROLL = 4
VROWS = 2 * HEAD_DIM
BIAS_PARTS = 3
BIAS_STRIDE = 8
MASKED_SCORE = -1e30
TRI_ROWS = TK + 16
FIRST_LIVE_TILE = 3
SKIP_DISTANCE = 3
DEAD_GAP = 140.0

F32 = jnp.float32
BF16 = jnp.bfloat16


def _split3(x):
    hi = x.astype(BF16)
    r1 = x - hi.astype(F32)
    mid = r1.astype(BF16)
    lo = (r1 - mid.astype(F32)).astype(BF16)
    return hi, mid, lo


def _log_sigmoid(x):
    return jnp.minimum(x, 0.0) - jnp.log1p(jnp.exp(-jnp.abs(x)))


def _adaln_kernel(c_ref, w_ref, b_ref, o_ref):
    c = c_ref[...]
    s = c * jax.nn.sigmoid(c)
    o_ref[...] = jnp.dot(s, w_ref[...], preferred_element_type=F32,
                         precision=lax.Precision.HIGHEST) + b_ref[...]


def _adaln(c, w_ada, b_ada):
    b, d = c.shape
    n = w_ada.shape[1]
    return pl.pallas_call(
        _adaln_kernel,
        grid=(n // ADA_COLS,),
        in_specs=[pl.BlockSpec((b, d), lambda j: (0, 0)),
                  pl.BlockSpec((d, ADA_COLS), lambda j: (0, j)),
                  pl.BlockSpec((1, ADA_COLS), lambda j: (0, j))],
        out_specs=pl.BlockSpec((b, ADA_COLS), lambda j: (0, j)),
        out_shape=jax.ShapeDtypeStruct((b, n), F32),
        name="adaln",
    )(c, w_ada, b_ada.reshape(1, n))


def _modulated_norm(x_ref, shift_ref, scale_ref, g_ref):
    x = x_ref[0]
    y = x * lax.rsqrt(jnp.mean(x * x, axis=-1, keepdims=True) + NORM_EPS) * g_ref[...]
    return (y * (1.0 + scale_ref[0]) + shift_ref[0]).astype(BF16)


def _project(h, wkz_ref, wqvt_ref, qt_ref, k_ref, vt_ref, z_ref):
    chunks = D_INNER // COL_CHUNK
    for o, out_ref in enumerate((k_ref, z_ref)):
        for cc in range(chunks):
            col = o * D_INNER + cc * COL_CHUNK
            out_ref[0, :, cc * COL_CHUNK:(cc + 1) * COL_CHUNK] = jnp.dot(
                h, wkz_ref[:, col:col + COL_CHUNK], preferred_element_type=F32).astype(BF16)
    for o, (out_ref, scale) in enumerate(((qt_ref, QK_SCALE), (vt_ref, None))):
        for cc in range(chunks):
            row = o * D_INNER + cc * COL_CHUNK
            t = lax.dot_general(wqvt_ref[row:row + COL_CHUNK, :], h, (((1,), (1,)), ((), ())),
                                preferred_element_type=F32)
            if scale is not None:
                t = t * scale
            out_ref[0, cc * COL_CHUNK:(cc + 1) * COL_CHUNK, :] = t.astype(BF16)


def _inproj_sb_kernel(x_ref, shift_ref, scale_ref, g_ref, wkz_ref, wqvt_ref,
                      qt_ref, k_ref, vt_ref, z_ref):
    h = _modulated_norm(x_ref, shift_ref, scale_ref, g_ref)
    _project(h, wkz_ref, wqvt_ref, qt_ref, k_ref, vt_ref, z_ref)


def _inproj_fox_kernel(x_ref, shift_ref, scale_ref, g_ref, wkz_ref, wqvt_ref, wf_ref, bf_ref,
                       qt_ref, k_ref, vt_ref, z_ref, hi_ref, mid_ref, lo_ref, carry_ref):
    h = _modulated_norm(x_ref, shift_ref, scale_ref, g_ref)
    _project(h, wkz_ref, wqvt_ref, qt_ref, k_ref, vt_ref, z_ref)

    @pl.when(pl.program_id(1) == 0)
    def _():
        carry_ref[...] = jnp.zeros_like(carry_ref)

    log_f = _log_sigmoid(jnp.dot(h, wf_ref[...], preferred_element_type=F32) + bf_ref[...])
    rows = log_f.shape[0]
    r = lax.broadcasted_iota(jnp.int32, (rows, rows), 0)
    c = lax.broadcasted_iota(jnp.int32, (rows, rows), 1)
    tri = (c <= r).astype(BF16)
    hi, mid, lo = _split3(log_f)
    cum = (jnp.dot(tri, hi, preferred_element_type=F32)
           + jnp.dot(tri, mid, preferred_element_type=F32)
           + jnp.dot(tri, lo, preferred_element_type=F32)) + carry_ref[...]
    carry_ref[...] = cum[rows - 1:rows, :]
    hi_ref[0], mid_ref[0], lo_ref[0] = _split3(cum * (-LOG2E))


def _inproj(x, mod3, g, wkz, wqvt, forget=None):
    b, s, d = x.shape
    row_spec = lambda width: pl.BlockSpec((1, ROW_TILE, width), lambda i, j: (i, j, 0))
    col_spec = pl.BlockSpec((1, D_INNER, ROW_TILE), lambda i, j: (i, 0, j))
    mod_spec = lambda k: pl.BlockSpec((1, 1, d), lambda i, j: (3 * i + k, 0, 0))
    const = lambda shape: pl.BlockSpec(shape, lambda i, j: (0,) * len(shape),
                                       pipeline_mode=pl.Buffered(1))
    in_specs = [row_spec(d), mod_spec(0), mod_spec(1), const((1, d)),
                const(wkz.shape), const(wqvt.shape)]
    args = [x, mod3, mod3, g.reshape(1, d), wkz, wqvt]
    row_major = jax.ShapeDtypeStruct((b, s, D_INNER), BF16)
    col_major = jax.ShapeDtypeStruct((b, D_INNER, s), BF16)
    out_specs = [col_spec, row_spec(D_INNER), col_spec, row_spec(D_INNER)]
    out_shape = [col_major, row_major, col_major, row_major]
    scratch = []
    kernel = _inproj_sb_kernel
    if forget is not None:
        wf, bf = forget
        in_specs += [const(wf.shape), const(bf.shape)]
        args += [wf, bf]
        out_specs += [row_spec(LANES)] * BIAS_PARTS
        out_shape += [jax.ShapeDtypeStruct((b, s, LANES), BF16)] * BIAS_PARTS
        scratch = [pltpu.VMEM((1, LANES), F32)]
        kernel = _inproj_fox_kernel
    return pl.pallas_call(
        kernel,
        grid=(b, s // ROW_TILE),
        in_specs=in_specs,
        out_specs=out_specs,
        out_shape=out_shape,
        scratch_shapes=scratch,
        compiler_params=pltpu.CompilerParams(
            dimension_semantics=("parallel", "arbitrary"), vmem_limit_bytes=VMEM_LIMIT),
        name="inproj_fox" if forget is not None else "inproj_sb",
    )(*args)


def _projection_weights(w_in):
    q, k, v, z = (w_in[:, i * D_INNER:(i + 1) * D_INNER] for i in range(4))
    return (jnp.concatenate([k, z], axis=1).astype(BF16),
            jnp.concatenate([q, v], axis=1).T.astype(BF16))


def _mask_keys_to_head(k, h):
    lane = lax.broadcasted_iota(jnp.int32, k.shape, 1)
    return jnp.where(lane // HEAD_DIM == h, k, jnp.zeros_like(k))


def _finalize(z_ref, o_ref, acc_ref, nq, normalize):
    def body(i, _):
        outs = []
        for h in range(HEADS_PER_BLOCK):
            acc = acc_ref[i, h]
            o = acc[:HEAD_DIM]
            if normalize:
                o = o / acc[HEAD_DIM:HEAD_DIM + 1]
            outs.append(o)
        o = jnp.concatenate(outs, axis=0).T
        z = z_ref[0, pl.ds(i * TQ, TQ), :].astype(F32)
        o_ref[0, pl.ds(i * TQ, TQ), :] = (o * (z * jax.nn.sigmoid(z))).astype(o_ref.dtype)
        return 0

    lax.fori_loop(0, nq, body, 0)


def _wave_pair(n, nq):
    n = jnp.clip(n, 0, nq * (nq + 1) // 2 - 1)
    d = jnp.int32(0)
    for e in range(1, nq):
        d = d + (n >= e * nq - e * (e - 1) // 2).astype(jnp.int32)
    qi = n - (d * nq - d * (d - 1) // 2) + d
    return qi, qi - d


def _fox_kernel(qt_ref, k_ref, bp_ref, vt_ref, z_ref, o_ref,
                kx_ref, w_ref, vx_ref, mk_ref, s0_ref, s1_ref, p0_ref, p1_ref, m_ref, acc_ref, bnd_ref):
    s_len = k_ref.shape[1]
    nq = s_len // TQ
    n_pairs = nq * (nq + 1) // 2
    hp = pl.program_id(1)
    k = k_ref[0]
    bp = bp_ref[0]
    lane = lax.broadcasted_iota(jnp.int32, bp.shape, 1)
    for h in range(HEADS_PER_BLOCK):
        base = BIAS_STRIDE * hp + BIAS_PARTS * h
        kx_ref[h, :, :LANES] = _mask_keys_to_head(k, h)
        kx_ref[h, :, LANES:] = jnp.where((lane >= base) & (lane < base + BIAS_PARTS),
                                         bp, jnp.zeros_like(bp))
        vx_ref[h, :HEAD_DIM, :] = vt_ref[0, h * HEAD_DIM:(h + 1) * HEAD_DIM, :]
        vx_ref[h, HEAD_DIM:, :] = jnp.ones((VROWS - HEAD_DIM, s_len), BF16)
    w_ref[:LANES, :] = qt_ref[0]
    w_ref[LANES:, :] = jnp.ones((LANES, s_len), BF16)
    k_abs = jnp.max(jnp.abs(k.astype(F32)), axis=0, keepdims=True)
    lane1 = lax.broadcasted_iota(jnp.int32, (8, LANES), 1)
    k_abs = jnp.concatenate([jnp.where(lane1 // HEAD_DIM == h, k_abs, 0.0)
                             for h in range(HEADS_PER_BLOCK)], axis=0).astype(BF16)
    bound = jnp.dot(k_abs, jnp.abs(qt_ref[0]), preferred_element_type=F32)
    for h in range(HEADS_PER_BLOCK):
        bnd_ref[h] = bound[8 * h:8 * h + 1]
    key = lax.broadcasted_iota(jnp.int32, (TK, TQ), 0)
    qry = lax.broadcasted_iota(jnp.int32, (TK, TQ), 1)
    mk_ref[0] = jnp.zeros((TK, TQ), F32)
    mk_ref[1] = jnp.where(key <= qry, 0.0, MASKED_SCORE)
    m_ref[...] = jnp.full(m_ref.shape, MASKED_SCORE, F32)
    acc_ref[...] = jnp.zeros_like(acc_ref)
    p1_ref[...] = jnp.zeros_like(p1_ref)
    s_bufs, p_bufs = (s0_ref, s1_ref), (p0_ref, p1_ref)

    def score_stage(n, s_ref):
        qi, kj = _wave_pair(n, nq)
        w = w_ref[:, pl.ds(qi * TQ, TQ)]
        for h in range(HEADS_PER_BLOCK):
            s_ref[h] = jnp.dot(kx_ref[h, pl.ds(kj * TK, TK), :], w, preferred_element_type=F32)

    def softmax_stage(n, s_ref, p_ref, last):
        qi, kj = _wave_pair(n, nq)
        diagonal = (qi == kj).astype(jnp.int32)
        m_out, al_out = [], []
        for h in range(HEADS_PER_BLOCK):
            st = s_ref[h] + mk_ref[diagonal]
            m = jnp.where(qi == last[0], last[1][h], m_ref[qi, h])
            m_new = jnp.maximum(m, jnp.max(st, axis=0, keepdims=True))
            al_out.append(jnp.exp2(m - m_new))
            p_ref[h] = jnp.exp2(st - m_new).astype(BF16)
            m_ref[qi, h] = m_new
            m_out.append(m_new)
        return (qi, m_out), al_out

    def value_stage(n, p_ref, al):
        qi, kj = _wave_pair(n, nq)
        for h in range(HEADS_PER_BLOCK):
            pv = jnp.dot(vx_ref[h, :, pl.ds(kj * TK, TK)], p_ref[h], preferred_element_type=F32)
            acc_ref[qi, h] = al[h] * acc_ref[qi, h] + pv

    def steps(it, state):
        al_prv, last = state
        for u in range(UNROLL):
            n = it * UNROLL + u
            score_stage(n + 1, s_bufs[1 - u % 2])
            last, al = softmax_stage(n, s_bufs[u % 2], p_bufs[u % 2], last)
            value_stage(n - 1, p_bufs[1 - u % 2], al_prv)
            al_prv = al
        return al_prv, last

    score_stage(0, s_bufs[0])
    row = lambda value: [jnp.full((1, TQ), value, F32)] * HEADS_PER_BLOCK
    n_static = min(-(-(SKIP_DISTANCE * nq - SKIP_DISTANCE * (SKIP_DISTANCE - 1) // 2) // UNROLL),
                   n_pairs // UNROLL)
    state = lax.fori_loop(0, n_static, steps, (row(1.0), (jnp.int32(-1), row(0.0))))

    def live(carry):
        it = carry[0]
        qi0, kj0 = _wave_pair(it * UNROLL, nq)
        nearest = qi0 - kj0
        worst = jnp.float32(-jnp.inf)
        for i in range(SKIP_DISTANCE, nq):
            last_rows = (jnp.maximum(i - nearest, 0) + 1) * TK - 16
            for h in range(HEADS_PER_BLOCK):
                bias_end = jnp.sum(kx_ref[h, pl.ds(last_rows, 16), LANES:][15:].astype(F32))
                gap = jnp.max(bnd_ref[h, :, i * TQ:(i + 1) * TQ] - m_ref[i, h])
                worst = jnp.maximum(worst, jnp.where(i >= nearest, bias_end + gap, -jnp.inf))
        return (it < n_pairs // UNROLL) & (worst > -DEAD_GAP)

    def body(carry):
        it, state = carry
        return it + 1, steps(it, state)

    it, (al_prv, _) = lax.while_loop(live, body, (jnp.int32(n_static), state))
    value_stage(it * UNROLL - 1, p_bufs[1], al_prv)
    _finalize(z_ref, o_ref, acc_ref, nq, True)


def _sb_kernel(qt_ref, k_ref, vt_ref, z_ref, tri_ref, o_ref,
               kx_ref, mk_ref, s0_ref, s1_ref, lk0_ref, lk1_ref, lb0_ref, lb1_ref,
               p0_ref, p1_ref, r_ref, acc_ref):
    s_len = k_ref.shape[1]
    nq = s_len // TQ
    n_pairs = nq * (nq + 1) // 2
    k = k_ref[0]
    for h in range(HEADS_PER_BLOCK):
        kx_ref[h] = _mask_keys_to_head(k, h)
    key = lax.broadcasted_iota(jnp.int32, (TK, TQ), 0)
    qry = lax.broadcasted_iota(jnp.int32, (TK, TQ), 1)
    mk_ref[0] = jnp.zeros((TK, TQ), F32)
    mk_ref[1] = jnp.where(key < qry, 0.0, MASKED_SCORE)
    r_ref[...] = jnp.zeros_like(r_ref)
    acc_ref[...] = jnp.zeros_like(acc_ref)
    p1_ref[...] = jnp.zeros_like(p1_ref)
    s_bufs, lk_bufs, lb_bufs, p_bufs = ((s0_ref, s1_ref), (lk0_ref, lk1_ref),
                                        (lb0_ref, lb1_ref), (p0_ref, p1_ref))

    pair = lambda n: _wave_pair(n, nq)

    def score_stage(n, s_ref):
        qi, kj = pair(n)
        w = qt_ref[0, :, pl.ds(qi * TQ, TQ)]
        for h in range(HEADS_PER_BLOCK):
            s_ref[h] = jnp.dot(kx_ref[h, pl.ds(kj * TK, TK), :], w, preferred_element_type=F32)

    def gate_stage(n, s_ref, lk_ref, lb_ref):
        qi, kj = pair(n)
        diagonal = (qi == kj).astype(jnp.int32)
        for h in range(HEADS_PER_BLOCK):
            zz = s_ref[h] + mk_ref[diagonal]
            neg_abs = lax.bitcast_convert_type(
                lax.bitcast_convert_type(zz, jnp.uint32) | jnp.uint32(0x80000000), F32)
            lb = jnp.minimum(zz, 0.0) - jnp.log2(1.0 + jnp.exp2(neg_abs))
            lk_ref[h] = (lb - zz).astype(BF16)
            lb_ref[h] = lb

    def weight_stage(lk_ref, lb_ref, p_ref):
        tot = []
        for h in range(HEADS_PER_BLOCK):
            after = jnp.dot(tri_ref[...], lk_ref[h], preferred_element_type=F32)
            p_ref[h] = jnp.exp2(lb_ref[h] + after[:TK]).astype(BF16)
            tot.append(after[TK:TK + 1])
        return tot

    def value_stage(n, p_ref, tot, last):
        qi, kj = pair(n)
        r_out = []
        for h in range(HEADS_PER_BLOCK):
            pv = jnp.dot(vt_ref[0, h * HEAD_DIM:(h + 1) * HEAD_DIM, pl.ds(kj * TK, TK)], p_ref[h],
                         preferred_element_type=F32)
            r = jnp.where(qi == last[0], last[1][h], r_ref[qi, h])
            acc_ref[qi, h] += jnp.exp2(r) * pv
            r_ref[qi, h] = r + tot[h]
            r_out.append(r + tot[h])
        return qi, r_out

    def steps(it, state):
        tot_prv, last = state
        for u in range(UNROLL):
            n = it * UNROLL + u
            score_stage(n + 2, s_bufs[u % 2])
            gate_stage(n + 1, s_bufs[1 - u % 2], lk_bufs[1 - u % 2], lb_bufs[1 - u % 2])
            tot = weight_stage(lk_bufs[u % 2], lb_bufs[u % 2], p_bufs[u % 2])
            last = value_stage(n - 1, p_bufs[1 - u % 2], tot_prv, last)
            tot_prv = tot
        return tot_prv, last

    score_stage(0, s_bufs[0])
    gate_stage(0, s_bufs[0], lk_bufs[0], lb_bufs[0])
    score_stage(1, s_bufs[1])

    n_static = min(-(-(2 * nq - 1) // UNROLL), n_pairs // UNROLL)
    zero_row = [jnp.zeros((1, TQ), F32)] * HEADS_PER_BLOCK
    state = lax.fori_loop(0, n_static, steps, (zero_row, (jnp.int32(-1), zero_row)))

    def live(carry):
        it = carry[0]
        if nq <= FIRST_LIVE_TILE:
            return it < 0
        return (it < n_pairs // UNROLL) & (jnp.max(jnp.exp2(r_ref[FIRST_LIVE_TILE:])) > 0.0)

    def body(carry):
        it, state = carry
        return it + 1, steps(it, state)

    it, (tot_prv, last) = lax.while_loop(live, body, (jnp.int32(n_static), state))
    value_stage(it * UNROLL - 1, p_bufs[1], tot_prv, last)
    _finalize(z_ref, o_ref, acc_ref, nq, False)


def _mixer_call(kernel, name, args, in_specs, scratch, b, s):
    seq_spec = pl.BlockSpec((1, s, LANES), lambda i, j: (i, 0, j))
    return pl.pallas_call(
        kernel,
        grid=(b, D_INNER // LANES),
        in_specs=in_specs,
        out_specs=seq_spec,
        out_shape=jax.ShapeDtypeStruct((b, s, D_INNER), BF16),
        scratch_shapes=scratch,
        compiler_params=pltpu.CompilerParams(
            dimension_semantics=("parallel", "parallel"), vmem_limit_bytes=VMEM_LIMIT),
        name=name,
    )(*args)


def _fox_mixer(qt, k, bp, vt, z):
    b, s, _ = k.shape
    nq = s // TQ
    assert (nq * (nq + 1) // 2) % UNROLL == 0
    seq_spec = pl.BlockSpec((1, s, LANES), lambda i, j: (i, 0, j))
    t_spec = pl.BlockSpec((1, LANES, s), lambda i, j: (i, j, 0))
    bp_spec = pl.BlockSpec((1, s, LANES), lambda i, j: (i, 0, 0))
    tile = (HEADS_PER_BLOCK, TK, TQ)
    scratch = [pltpu.VMEM((HEADS_PER_BLOCK, s, 2 * LANES), BF16),
               pltpu.VMEM((2 * LANES, s), BF16),
               pltpu.VMEM((HEADS_PER_BLOCK, VROWS, s), BF16),
               pltpu.VMEM((2, TK, TQ), F32),
               pltpu.VMEM(tile, F32), pltpu.VMEM(tile, F32),
               pltpu.VMEM(tile, BF16), pltpu.VMEM(tile, BF16),
               pltpu.VMEM((nq, HEADS_PER_BLOCK, 1, TQ), F32),
               pltpu.VMEM((nq, HEADS_PER_BLOCK, VROWS, TQ), F32),
               pltpu.VMEM((HEADS_PER_BLOCK, 1, s), F32)]
    return _mixer_call(_fox_kernel, "fox", (qt, k, bp, vt, z),
                       [t_spec, seq_spec, bp_spec, t_spec, seq_spec], scratch, b, s)


def _sb_mixer(qt, k, vt, z):
    b, s, _ = k.shape
    nq = s // TQ
    assert (nq * (nq + 1) // 2) % UNROLL == 0
    r = lax.broadcasted_iota(jnp.int32, (TRI_ROWS, TK), 0)
    c = lax.broadcasted_iota(jnp.int32, (TRI_ROWS, TK), 1)
    tri = ((c > r) | (r >= TK)).astype(BF16)
    seq_spec = pl.BlockSpec((1, s, LANES), lambda i, j: (i, 0, j))
    t_spec = pl.BlockSpec((1, LANES, s), lambda i, j: (i, j, 0))
    tri_spec = pl.BlockSpec((TRI_ROWS, TK), lambda i, j: (0, 0))
    tile = (HEADS_PER_BLOCK, TK, TQ)
    scratch = [pltpu.VMEM((HEADS_PER_BLOCK, s, LANES), BF16),
               pltpu.VMEM((2, TK, TQ), F32),
               pltpu.VMEM(tile, F32), pltpu.VMEM(tile, F32),
               pltpu.VMEM(tile, BF16), pltpu.VMEM(tile, BF16),
               pltpu.VMEM(tile, F32), pltpu.VMEM(tile, F32),
               pltpu.VMEM(tile, BF16), pltpu.VMEM(tile, BF16),
               pltpu.VMEM((nq, HEADS_PER_BLOCK, 1, TQ), F32),
               pltpu.VMEM((nq, HEADS_PER_BLOCK, HEAD_DIM, TQ), F32)]
    return _mixer_call(_sb_kernel, "sb", (qt, k, vt, z, tri),
                       [t_spec, seq_spec, t_spec, seq_spec, tri_spec], scratch, b, s)


def _outproj_kernel(u_ref, w_ref, x_ref, gate_ref, o_ref):
    y = jnp.dot(u_ref[0], w_ref[...], preferred_element_type=F32)
    o_ref[0] = x_ref[0] + gate_ref[0] * y


def _outproj_final_kernel(u_ref, w_ref, x_ref, gate_ref, g_ref, o_ref):
    y = jnp.dot(u_ref[0], w_ref[...], preferred_element_type=F32)
    r = x_ref[0] + gate_ref[0] * y
    o_ref[0] = r * lax.rsqrt(jnp.mean(r * r, axis=-1, keepdims=True) + NORM_EPS) * g_ref[...]


def _outproj(u, w, x, mod3, final_g=None):
    b, s, d = x.shape
    row_spec = lambda width: pl.BlockSpec((1, ROW_TILE, width), lambda i, j: (i, j, 0))
    const = lambda shape: pl.BlockSpec(shape, lambda i, j: (0,) * len(shape),
                                       pipeline_mode=pl.Buffered(1))
    in_specs = [row_spec(D_INNER), const(w.shape), row_spec(d),
                pl.BlockSpec((1, 1, d), lambda i, j: (3 * i + 2, 0, 0))]
    args = [u, w, x, mod3]
    kernel = _outproj_kernel
    if final_g is not None:
        in_specs.append(const((1, d)))
        args.append(final_g.reshape(1, d))
        kernel = _outproj_final_kernel
    return pl.pallas_call(
        kernel,
        grid=(b, s // ROW_TILE),
        in_specs=in_specs,
        out_specs=row_spec(d),
        out_shape=jax.ShapeDtypeStruct((b, s, d), F32),
        compiler_params=pltpu.CompilerParams(
            dimension_semantics=("parallel", "parallel"), vmem_limit_bytes=VMEM_LIMIT),
        name="outproj_final" if final_g is not None else "outproj",
    )(*args)


def _arrange_bias_parts(parts):
    b, s, _ = parts[0].shape
    x = jnp.stack([p[:, :, :N_HEADS] for p in parts], axis=-1)
    x = x.reshape(b, s, N_HEADS // HEADS_PER_BLOCK, HEADS_PER_BLOCK * BIAS_PARTS)
    x = jnp.pad(x, ((0, 0), (0, 0), (0, 0), (0, BIAS_STRIDE - HEADS_PER_BLOCK * BIAS_PARTS)))
    return x.reshape(b, s, LANES)


def _fox_layer(x, c, norm_g, w_ada, b_ada, w_in, b_f, w_out, final_g):
    b, s, d = x.shape
    mod3 = _adaln(c, w_ada, b_ada).reshape(b * 3, 1, d)
    wkz, wqvt = _projection_weights(w_in)
    wf = jnp.pad(w_in[:, 4 * D_INNER:], ((0, 0), (0, LANES - N_HEADS))).astype(BF16)
    bf = jnp.pad(b_f, (0, LANES - N_HEADS)).reshape(1, LANES)
    qt, k, vt, z, hi, mid, lo = _inproj(x, mod3, norm_g, wkz, wqvt, forget=(wf, bf))
    u = _fox_mixer(qt, k, _arrange_bias_parts((hi, mid, lo)), vt, z)
    return _outproj(u, w_out.astype(BF16), x, mod3, final_g)


def _sb_layer(x, c, norm_g, w_ada, b_ada, w_in, w_out, final_g):
    b, s, d = x.shape
    mod3 = _adaln(c, w_ada, b_ada).reshape(b * 3, 1, d)
    wkz, wqvt = _projection_weights(w_in)
    qt, k, vt, z = _inproj(x, mod3, norm_g, wkz, wqvt)
    u = _sb_mixer(qt, k, vt, z)
    return _outproj(u, w_out.astype(BF16), x, mod3, final_g)


def kernel(x, c, fox_norm_g, fox_w_ada, fox_b_ada, fox_w_in, fox_b_f, fox_w_out,
           sb_norm_g, sb_w_ada, sb_b_ada, sb_w_in, sb_w_out, final_norm_g):
    depth = fox_norm_g.shape[0] + sb_norm_g.shape[0]
    for i in range(depth):
        j = i // 2
        final_g = final_norm_g if i == depth - 1 else None
        if i % 2 == 0:
            x = _fox_layer(x, c, fox_norm_g[j], fox_w_ada[j], fox_b_ada[j],
                           fox_w_in[j], fox_b_f[j], fox_w_out[j], final_g)
        else:
            x = _sb_layer(x, c, sb_norm_g[j], sb_w_ada[j], sb_b_ada[j],
                          sb_w_in[j], sb_w_out[j], final_g)
    return x
```

```python
import math

import jax
import jax.numpy as jnp
from jax import lax
from jax.experimental import pallas as pl
from jax.experimental.pallas import tpu as pltpu

D_MODEL = 1024
D_INNER = 2048
HEAD_DIM = 64
N_HEADS = D_INNER // HEAD_DIM
NORM_EPS = 1e-6
LOG2E = math.log2(math.e)
QK_SCALE = HEAD_DIM ** -0.5 * LOG2E

LANES = 128
HEADS_PER_BLOCK = LANES // HEAD_DIM
VMEM_LIMIT = 56 * 1024 * 1024

ROW_TILE = 512
COL_CHUNK = 512
ADA_COLS = 512
TQ = 256
TK = 256
UNROLL = 4
VROWS = 2 * HEAD_DIM
BIAS_PARTS = 3
BIAS_STRIDE = 8
MASKED_SCORE = -1e30
TRI_ROWS = TK + 16
FIRST_LIVE_TILE = 3
SKIP_DISTANCE = 3
DEAD_GAP = 140.0

F32 = jnp.float32
BF16 = jnp.bfloat16


def _split3(x):
    hi = x.astype(BF16)
    r1 = x - hi.astype(F32)
    mid = r1.astype(BF16)
    lo = (r1 - mid.astype(F32)).astype(BF16)
    return hi, mid, lo


def _log_sigmoid(x):
    return jnp.minimum(x, 0.0) - jnp.log1p(jnp.exp(-jnp.abs(x)))


def _adaln_kernel(c_ref, w_ref, b_ref, o_ref):
    c = c_ref[...]
    s = c * jax.nn.sigmoid(c)
    o_ref[...] = jnp.dot(s, w_ref[...], preferred_element_type=F32,
                         precision=lax.Precision.HIGHEST) + b_ref[...]


def _adaln(c, w_ada, b_ada):
    b, d = c.shape
    n = w_ada.shape[1]
    return pl.pallas_call(
        _adaln_kernel,
        grid=(n // ADA_COLS,),
        in_specs=[pl.BlockSpec((b, d), lambda j: (0, 0)),
                  pl.BlockSpec((d, ADA_COLS), lambda j: (0, j)),
                  pl.BlockSpec((1, ADA_COLS), lambda j: (0, j))],
        out_specs=pl.BlockSpec((b, ADA_COLS), lambda j: (0, j)),
        out_shape=jax.ShapeDtypeStruct((b, n), F32),
        name="adaln",
    )(c, w_ada, b_ada.reshape(1, n))


def _modulated_norm(x_ref, shift_ref, scale_ref, g_ref):
    x = x_ref[0]
    y = x * lax.rsqrt(jnp.mean(x * x, axis=-1, keepdims=True) + NORM_EPS) * g_ref[...]
    return (y * (1.0 + scale_ref[0]) + shift_ref[0]).astype(BF16)


def _project(h, wkz_ref, wqvt_ref, qt_ref, k_ref, vt_ref, z_ref):
    chunks = D_INNER // COL_CHUNK
    for o, out_ref in enumerate((k_ref, z_ref)):
        for cc in range(chunks):
            col = o * D_INNER + cc * COL_CHUNK
            out_ref[0, :, cc * COL_CHUNK:(cc + 1) * COL_CHUNK] = jnp.dot(
                h, wkz_ref[:, col:col + COL_CHUNK], preferred_element_type=F32).astype(BF16)
    for o, (out_ref, scale) in enumerate(((qt_ref, QK_SCALE), (vt_ref, None))):
        for cc in range(chunks):
            row = o * D_INNER + cc * COL_CHUNK
            t = lax.dot_general(wqvt_ref[row:row + COL_CHUNK, :], h, (((1,), (1,)), ((), ())),
                                preferred_element_type=F32)
            if scale is not None:
                t = t * scale
            out_ref[0, cc * COL_CHUNK:(cc + 1) * COL_CHUNK, :] = t.astype(BF16)


def _inproj_sb_kernel(x_ref, shift_ref, scale_ref, g_ref, wkz_ref, wqvt_ref,
                      qt_ref, k_ref, vt_ref, z_ref):
    h = _modulated_norm(x_ref, shift_ref, scale_ref, g_ref)
    _project(h, wkz_ref, wqvt_ref, qt_ref, k_ref, vt_ref, z_ref)


def _inproj_fox_kernel(x_ref, shift_ref, scale_ref, g_ref, wkz_ref, wqvt_ref, wf_ref, bf_ref,
                       qt_ref, k_ref, vt_ref, z_ref, hi_ref, mid_ref, lo_ref, carry_ref):
    h = _modulated_norm(x_ref, shift_ref, scale_ref, g_ref)
    _project(h, wkz_ref, wqvt_ref, qt_ref, k_ref, vt_ref, z_ref)

    @pl.when(pl.program_id(1) == 0)
    def _():
        carry_ref[...] = jnp.zeros_like(carry_ref)

    log_f = _log_sigmoid(jnp.dot(h, wf_ref[...], preferred_element_type=F32) + bf_ref[...])
    rows = log_f.shape[0]
    r = lax.broadcasted_iota(jnp.int32, (rows, rows), 0)
    c = lax.broadcasted_iota(jnp.int32, (rows, rows), 1)
    tri = (c <= r).astype(BF16)
    hi, mid, lo = _split3(log_f)
    cum = (jnp.dot(tri, hi, preferred_element_type=F32)
           + jnp.dot(tri, mid, preferred_element_type=F32)
           + jnp.dot(tri, lo, preferred_element_type=F32)) + carry_ref[...]
    carry_ref[...] = cum[rows - 1:rows, :]
    hi_ref[0], mid_ref[0], lo_ref[0] = _split3(cum * (-LOG2E))


def _inproj(x, mod3, g, wkz, wqvt, forget=None):
    b, s, d = x.shape
    row_spec = lambda width: pl.BlockSpec((1, ROW_TILE, width), lambda i, j: (i, j, 0))
    col_spec = pl.BlockSpec((1, D_INNER, ROW_TILE), lambda i, j: (i, 0, j))
    mod_spec = lambda k: pl.BlockSpec((1, 1, d), lambda i, j: (3 * i + k, 0, 0))
    const = lambda shape: pl.BlockSpec(shape, lambda i, j: (0,) * len(shape),
                                       pipeline_mode=pl.Buffered(1))
    in_specs = [row_spec(d), mod_spec(0), mod_spec(1), const((1, d)),
                const(wkz.shape), const(wqvt.shape)]
    args = [x, mod3, mod3, g.reshape(1, d), wkz, wqvt]
    row_major = jax.ShapeDtypeStruct((b, s, D_INNER), BF16)
    col_major = jax.ShapeDtypeStruct((b, D_INNER, s), BF16)
    out_specs = [col_spec, row_spec(D_INNER), col_spec, row_spec(D_INNER)]
    out_shape = [col_major, row_major, col_major, row_major]
    scratch = []
    kernel = _inproj_sb_kernel
    if forget is not None:
        wf, bf = forget
        in_specs += [const(wf.shape), const(bf.shape)]
        args += [wf, bf]
        out_specs += [row_spec(LANES)] * BIAS_PARTS
        out_shape += [jax.ShapeDtypeStruct((b, s, LANES), BF16)] * BIAS_PARTS
        scratch = [pltpu.VMEM((1, LANES), F32)]
        kernel = _inproj_fox_kernel
    return pl.pallas_call(
        kernel,
        grid=(b, s // ROW_TILE),
        in_specs=in_specs,
        out_specs=out_specs,
        out_shape=out_shape,
        scratch_shapes=scratch,
        compiler_params=pltpu.CompilerParams(
            dimension_semantics=("parallel", "arbitrary"), vmem_limit_bytes=VMEM_LIMIT),
        name="inproj_fox" if forget is not None else "inproj_sb",
    )(*args)


def _projection_weights(w_in):
    q, k, v, z = (w_in[:, i * D_INNER:(i + 1) * D_INNER] for i in range(4))
    return (jnp.concatenate([k, z], axis=1).astype(BF16),
            jnp.concatenate([q, v], axis=1).T.astype(BF16))


def _mask_keys_to_head(k, h):
    lane = lax.broadcasted_iota(jnp.int32, k.shape, 1)
    return jnp.where(lane // HEAD_DIM == h, k, jnp.zeros_like(k))


def _finalize(z_ref, o_ref, acc_ref, nq, normalize):
    for i in range(nq):
        outs = []
        for h in range(HEADS_PER_BLOCK):
            acc = acc_ref[i, h]
            o = acc[:HEAD_DIM]
            if normalize:
                o = o / acc[HEAD_DIM:HEAD_DIM + 1]
            outs.append(o)
        o = jnp.concatenate(outs, axis=0).T
        z = z_ref[0, i * TQ:(i + 1) * TQ, :].astype(F32)
        o_ref[0, i * TQ:(i + 1) * TQ, :] = (o * (z * jax.nn.sigmoid(z))).astype(o_ref.dtype)


def _wave_pair(n, nq):
    n = jnp.clip(n, 0, nq * (nq + 1) // 2 - 1)
    d = jnp.int32(0)
    for e in range(1, nq):
        d = d + (n >= e * nq - e * (e - 1) // 2).astype(jnp.int32)
    qi = n - (d * nq - d * (d - 1) // 2) + d
    return qi, qi - d


def _fox_kernel(qt_ref, k_ref, bp_ref, vt_ref, z_ref, o_ref,
                kx_ref, w_ref, vx_ref, mk_ref, s0_ref, s1_ref, p0_ref, p1_ref, m_ref, acc_ref, bnd_ref):
    s_len = k_ref.shape[1]
    nq = s_len // TQ
    n_pairs = nq * (nq + 1) // 2
    hp = pl.program_id(1)
    k = k_ref[0]
    bp = bp_ref[0]
    lane = lax.broadcasted_iota(jnp.int32, bp.shape, 1)
    for h in range(HEADS_PER_BLOCK):
        base = BIAS_STRIDE * hp + BIAS_PARTS * h
        kx_ref[h, :, :LANES] = _mask_keys_to_head(k, h)
        kx_ref[h, :, LANES:] = jnp.where((lane >= base) & (lane < base + BIAS_PARTS),
                                         bp, jnp.zeros_like(bp))
        vx_ref[h, :HEAD_DIM, :] = vt_ref[0, h * HEAD_DIM:(h + 1) * HEAD_DIM, :]
        vx_ref[h, HEAD_DIM:, :] = jnp.ones((VROWS - HEAD_DIM, s_len), BF16)
    w_ref[:LANES, :] = qt_ref[0]
    w_ref[LANES:, :] = jnp.ones((LANES, s_len), BF16)
    k_abs = jnp.max(jnp.abs(k.astype(F32)), axis=0, keepdims=True)
    lane1 = lax.broadcasted_iota(jnp.int32, (8, LANES), 1)
    k_abs = jnp.concatenate([jnp.where(lane1 // HEAD_DIM == h, k_abs, 0.0)
                             for h in range(HEADS_PER_BLOCK)], axis=0).astype(BF16)
    bound = jnp.dot(k_abs, jnp.abs(qt_ref[0]), preferred_element_type=F32)
    for h in range(HEADS_PER_BLOCK):
        bnd_ref[h] = bound[8 * h:8 * h + 1]
    key = lax.broadcasted_iota(jnp.int32, (TK, TQ), 0)
    qry = lax.broadcasted_iota(jnp.int32, (TK, TQ), 1)
    mk_ref[0] = jnp.zeros((TK, TQ), F32)
    mk_ref[1] = jnp.where(key <= qry, 0.0, MASKED_SCORE)
    m_ref[...] = jnp.full(m_ref.shape, MASKED_SCORE, F32)
    acc_ref[...] = jnp.zeros_like(acc_ref)
    p1_ref[...] = jnp.zeros_like(p1_ref)
    s_bufs, p_bufs = (s0_ref, s1_ref), (p0_ref, p1_ref)

    def score_stage(n, s_ref):
        qi, kj = _wave_pair(n, nq)
        w = w_ref[:, pl.ds(qi * TQ, TQ)]
        for h in range(HEADS_PER_BLOCK):
            s_ref[h] = jnp.dot(kx_ref[h, pl.ds(kj * TK, TK), :], w, preferred_element_type=F32)

    def softmax_stage(n, s_ref, p_ref, last):
        qi, kj = _wave_pair(n, nq)
        diagonal = (qi == kj).astype(jnp.int32)
        m_out, al_out = [], []
        for h in range(HEADS_PER_BLOCK):
            st = s_ref[h] + mk_ref[diagonal]
            m = jnp.where(qi == last[0], last[1][h], m_ref[qi, h])
            m_new = jnp.maximum(m, jnp.max(st, axis=0, keepdims=True))
            al_out.append(jnp.exp2(m - m_new))
            p_ref[h] = jnp.exp2(st - m_new).astype(BF16)
            m_ref[qi, h] = m_new
            m_out.append(m_new)
        return (qi, m_out), al_out

    def value_stage(n, p_ref, al):
        qi, kj = _wave_pair(n, nq)
        for h in range(HEADS_PER_BLOCK):
            pv = jnp.dot(vx_ref[h, :, pl.ds(kj * TK, TK)], p_ref[h], preferred_element_type=F32)
            acc_ref[qi, h] = al[h] * acc_ref[qi, h] + pv

    def steps(it, state):
        al_prv, last = state
        for u in range(UNROLL):
            n = it * UNROLL + u
            score_stage(n + 1, s_bufs[1 - u % 2])
            last, al = softmax_stage(n, s_bufs[u % 2], p_bufs[u % 2], last)
            value_stage(n - 1, p_bufs[1 - u % 2], al_prv)
            al_prv = al
        return al_prv, last

    score_stage(0, s_bufs[0])
    row = lambda value: [jnp.full((1, TQ), value, F32)] * HEADS_PER_BLOCK
    n_static = min(-(-(SKIP_DISTANCE * nq - SKIP_DISTANCE * (SKIP_DISTANCE - 1) // 2) // UNROLL),
                   n_pairs // UNROLL)
    state = lax.fori_loop(0, n_static, steps, (row(1.0), (jnp.int32(-1), row(0.0))))

    def live(carry):
        it = carry[0]
        qi0, kj0 = _wave_pair(it * UNROLL, nq)
        nearest = qi0 - kj0
        worst = jnp.float32(-jnp.inf)
        for i in range(SKIP_DISTANCE, nq):
            last_rows = (jnp.maximum(i - nearest, 0) + 1) * TK - 16
            for h in range(HEADS_PER_BLOCK):
                bias_end = jnp.sum(kx_ref[h, pl.ds(last_rows, 16), LANES:][15:].astype(F32))
                gap = jnp.max(bnd_ref[h, :, i * TQ:(i + 1) * TQ] - m_ref[i, h])
                worst = jnp.maximum(worst, jnp.where(i >= nearest, bias_end + gap, -jnp.inf))
        return (it < n_pairs // UNROLL) & (worst > -DEAD_GAP)

    def body(carry):
        it, state = carry
        return it + 1, steps(it, state)

    it, (al_prv, _) = lax.while_loop(live, body, (jnp.int32(n_static), state))
    value_stage(it * UNROLL - 1, p_bufs[1], al_prv)
    _finalize(z_ref, o_ref, acc_ref, nq, True)


def _sb_kernel(qt_ref, k_ref, vt_ref, z_ref, tri_ref, o_ref,
               kx_ref, mk_ref, s0_ref, s1_ref, lk0_ref, lk1_ref, lb0_ref, lb1_ref,
               p0_ref, p1_ref, r_ref, acc_ref):
    s_len = k_ref.shape[1]
    nq = s_len // TQ
    n_pairs = nq * (nq + 1) // 2
    k = k_ref[0]
    for h in range(HEADS_PER_BLOCK):
        kx_ref[h] = _mask_keys_to_head(k, h)
    key = lax.broadcasted_iota(jnp.int32, (TK, TQ), 0)
    qry = lax.broadcasted_iota(jnp.int32, (TK, TQ), 1)
    mk_ref[0] = jnp.zeros((TK, TQ), F32)
    mk_ref[1] = jnp.where(key < qry, 0.0, MASKED_SCORE)
    r_ref[...] = jnp.zeros_like(r_ref)
    acc_ref[...] = jnp.zeros_like(acc_ref)
    p1_ref[...] = jnp.zeros_like(p1_ref)
    s_bufs, lk_bufs, lb_bufs, p_bufs = ((s0_ref, s1_ref), (lk0_ref, lk1_ref),
                                        (lb0_ref, lb1_ref), (p0_ref, p1_ref))

    pair = lambda n: _wave_pair(n, nq)

    def score_stage(n, s_ref):
        qi, kj = pair(n)
        w = qt_ref[0, :, pl.ds(qi * TQ, TQ)]
        for h in range(HEADS_PER_BLOCK):
            s_ref[h] = jnp.dot(kx_ref[h, pl.ds(kj * TK, TK), :], w, preferred_element_type=F32)

    def gate_stage(n, s_ref, lk_ref, lb_ref):
        qi, kj = pair(n)
        diagonal = (qi == kj).astype(jnp.int32)
        for h in range(HEADS_PER_BLOCK):
            zz = s_ref[h] + mk_ref[diagonal]
            neg_abs = lax.bitcast_convert_type(
                lax.bitcast_convert_type(zz, jnp.uint32) | jnp.uint32(0x80000000), F32)
            lb = jnp.minimum(zz, 0.0) - jnp.log2(1.0 + jnp.exp2(neg_abs))
            lk_ref[h] = (lb - zz).astype(BF16)
            lb_ref[h] = lb

    def weight_stage(lk_ref, lb_ref, p_ref):
        tot = []
        for h in range(HEADS_PER_BLOCK):
            after = jnp.dot(tri_ref[...], lk_ref[h], preferred_element_type=F32)
            p_ref[h] = jnp.exp2(lb_ref[h] + after[:TK]).astype(BF16)
            tot.append(after[TK:TK + 1])
        return tot

    def value_stage(n, p_ref, tot, last):
        qi, kj = pair(n)
        r_out = []
        for h in range(HEADS_PER_BLOCK):
            pv = jnp.dot(vt_ref[0, h * HEAD_DIM:(h + 1) * HEAD_DIM, pl.ds(kj * TK, TK)], p_ref[h],
                         preferred_element_type=F32)
            r = jnp.where(qi == last[0], last[1][h], r_ref[qi, h])
            acc_ref[qi, h] += jnp.exp2(r) * pv
            r_ref[qi, h] = r + tot[h]
            r_out.append(r + tot[h])
        return qi, r_out

    def steps(it, state):
        tot_prv, last = state
        for u in range(UNROLL):
            n = it * UNROLL + u
            score_stage(n + 2, s_bufs[u % 2])
            gate_stage(n + 1, s_bufs[1 - u % 2], lk_bufs[1 - u % 2], lb_bufs[1 - u % 2])
            tot = weight_stage(lk_bufs[u % 2], lb_bufs[u % 2], p_bufs[u % 2])
            last = value_stage(n - 1, p_bufs[1 - u % 2], tot_prv, last)
            tot_prv = tot
        return tot_prv, last

    score_stage(0, s_bufs[0])
    gate_stage(0, s_bufs[0], lk_bufs[0], lb_bufs[0])
    score_stage(1, s_bufs[1])

    n_static = min(-(-(2 * nq - 1) // UNROLL), n_pairs // UNROLL)
    zero_row = [jnp.zeros((1, TQ), F32)] * HEADS_PER_BLOCK
    state = lax.fori_loop(0, n_static, steps, (zero_row, (jnp.int32(-1), zero_row)))

    def live(carry):
        it = carry[0]
        if nq <= FIRST_LIVE_TILE:
            return it < 0
        return (it < n_pairs // UNROLL) & (jnp.max(jnp.exp2(r_ref[FIRST_LIVE_TILE:])) > 0.0)

    def body(carry):
        it, state = carry
        return it + 1, steps(it, state)

    it, (tot_prv, last) = lax.while_loop(live, body, (jnp.int32(n_static), state))
    value_stage(it * UNROLL - 1, p_bufs[1], tot_prv, last)
    _finalize(z_ref, o_ref, acc_ref, nq, False)


def _mixer_call(kernel, name, args, in_specs, scratch, b, s):
    seq_spec = pl.BlockSpec((1, s, LANES), lambda i, j: (i, 0, j))
    return pl.pallas_call(
        kernel,
        grid=(b, D_INNER // LANES),
        in_specs=in_specs,
        out_specs=seq_spec,
        out_shape=jax.ShapeDtypeStruct((b, s, D_INNER), BF16),
        scratch_shapes=scratch,
        compiler_params=pltpu.CompilerParams(
            dimension_semantics=("parallel", "parallel"), vmem_limit_bytes=VMEM_LIMIT),
        name=name,
    )(*args)


def _fox_mixer(qt, k, bp, vt, z):
    b, s, _ = k.shape
    nq = s // TQ
    assert (nq * (nq + 1) // 2) % UNROLL == 0
    seq_spec = pl.BlockSpec((1, s, LANES), lambda i, j: (i, 0, j))
    t_spec = pl.BlockSpec((1, LANES, s), lambda i, j: (i, j, 0))
    bp_spec = pl.BlockSpec((1, s, LANES), lambda i, j: (i, 0, 0))
    tile = (HEADS_PER_BLOCK, TK, TQ)
    scratch = [pltpu.VMEM((HEADS_PER_BLOCK, s, 2 * LANES), BF16),
               pltpu.VMEM((2 * LANES, s), BF16),
               pltpu.VMEM((HEADS_PER_BLOCK, VROWS, s), BF16),
               pltpu.VMEM((2, TK, TQ), F32),
               pltpu.VMEM(tile, F32), pltpu.VMEM(tile, F32),
               pltpu.VMEM(tile, BF16), pltpu.VMEM(tile, BF16),
               pltpu.VMEM((nq, HEADS_PER_BLOCK, 1, TQ), F32),
               pltpu.VMEM((nq, HEADS_PER_BLOCK, VROWS, TQ), F32),
               pltpu.VMEM((HEADS_PER_BLOCK, 1, s), F32)]
    return _mixer_call(_fox_kernel, "fox", (qt, k, bp, vt, z),
                       [t_spec, seq_spec, bp_spec, t_spec, seq_spec], scratch, b, s)


def _sb_mixer(qt, k, vt, z):
    b, s, _ = k.shape
    nq = s // TQ
    assert (nq * (nq + 1) // 2) % UNROLL == 0
    r = lax.broadcasted_iota(jnp.int32, (TRI_ROWS, TK), 0)
    c = lax.broadcasted_iota(jnp.int32, (TRI_ROWS, TK), 1)
    tri = ((c > r) | (r >= TK)).astype(BF16)
    seq_spec = pl.BlockSpec((1, s, LANES), lambda i, j: (i, 0, j))
    t_spec = pl.BlockSpec((1, LANES, s), lambda i, j: (i, j, 0))
    tri_spec = pl.BlockSpec((TRI_ROWS, TK), lambda i, j: (0, 0))
    tile = (HEADS_PER_BLOCK, TK, TQ)
    scratch = [pltpu.VMEM((HEADS_PER_BLOCK, s, LANES), BF16),
               pltpu.VMEM((2, TK, TQ), F32),
               pltpu.VMEM(tile, F32), pltpu.VMEM(tile, F32),
               pltpu.VMEM(tile, BF16), pltpu.VMEM(tile, BF16),
               pltpu.VMEM(tile, F32), pltpu.VMEM(tile, F32),
               pltpu.VMEM(tile, BF16), pltpu.VMEM(tile, BF16),
               pltpu.VMEM((nq, HEADS_PER_BLOCK, 1, TQ), F32),
               pltpu.VMEM((nq, HEADS_PER_BLOCK, HEAD_DIM, TQ), F32)]
    return _mixer_call(_sb_kernel, "sb", (qt, k, vt, z, tri),
                       [t_spec, seq_spec, t_spec, seq_spec, tri_spec], scratch, b, s)


def _outproj_kernel(u_ref, w_ref, x_ref, gate_ref, o_ref):
    y = jnp.dot(u_ref[0], w_ref[...], preferred_element_type=F32)
    o_ref[0] = x_ref[0] + gate_ref[0] * y


def _outproj_final_kernel(u_ref, w_ref, x_ref, gate_ref, g_ref, o_ref):
    y = jnp.dot(u_ref[0], w_ref[...], preferred_element_type=F32)
    r = x_ref[0] + gate_ref[0] * y
    o_ref[0] = r * lax.rsqrt(jnp.mean(r * r, axis=-1, keepdims=True) + NORM_EPS) * g_ref[...]


def _outproj(u, w, x, mod3, final_g=None):
    b, s, d = x.shape
    row_spec = lambda width: pl.BlockSpec((1, ROW_TILE, width), lambda i, j: (i, j, 0))
    const = lambda shape: pl.BlockSpec(shape, lambda i, j: (0,) * len(shape),
                                       pipeline_mode=pl.Buffered(1))
    in_specs = [row_spec(D_INNER), const(w.shape), row_spec(d),
                pl.BlockSpec((1, 1, d), lambda i, j: (3 * i + 2, 0, 0))]
    args = [u, w, x, mod3]
    kernel = _outproj_kernel
    if final_g is not None:
        in_specs.append(const((1, d)))
        args.append(final_g.reshape(1, d))
        kernel = _outproj_final_kernel
    return pl.pallas_call(
        kernel,
        grid=(b, s // ROW_TILE),
        in_specs=in_specs,
        out_specs=row_spec(d),
        out_shape=jax.ShapeDtypeStruct((b, s, d), F32),
        compiler_params=pltpu.CompilerParams(
            dimension_semantics=("parallel", "parallel"), vmem_limit_bytes=VMEM_LIMIT),
        name="outproj_final" if final_g is not None else "outproj",
    )(*args)


def _arrange_bias_parts(parts):
    b, s, _ = parts[0].shape
    x = jnp.stack([p[:, :, :N_HEADS] for p in parts], axis=-1)
    x = x.reshape(b, s, N_HEADS // HEADS_PER_BLOCK, HEADS_PER_BLOCK * BIAS_PARTS)
    x = jnp.pad(x, ((0, 0), (0, 0), (0, 0), (0, BIAS_STRIDE - HEADS_PER_BLOCK * BIAS_PARTS)))
    return x.reshape(b, s, LANES)


def _fox_layer(x, c, norm_g, w_ada, b_ada, w_in, b_f, w_out, final_g):
    b, s, d = x.shape
    mod3 = _adaln(c, w_ada, b_ada).reshape(b * 3, 1, d)
    wkz, wqvt = _projection_weights(w_in)
    wf = jnp.pad(w_in[:, 4 * D_INNER:], ((0, 0), (0, LANES - N_HEADS))).astype(BF16)
    bf = jnp.pad(b_f, (0, LANES - N_HEADS)).reshape(1, LANES)
    qt, k, vt, z, hi, mid, lo = _inproj(x, mod3, norm_g, wkz, wqvt, forget=(wf, bf))
    u = _fox_mixer(qt, k, _arrange_bias_parts((hi, mid, lo)), vt, z)
    return _outproj(u, w_out.astype(BF16), x, mod3, final_g)


def _sb_layer(x, c, norm_g, w_ada, b_ada, w_in, w_out, final_g):
    b, s, d = x.shape
    mod3 = _adaln(c, w_ada, b_ada).reshape(b * 3, 1, d)
    wkz, wqvt = _projection_weights(w_in)
    qt, k, vt, z = _inproj(x, mod3, norm_g, wkz, wqvt)
    u = _sb_mixer(qt, k, vt, z)
    return _outproj(u, w_out.astype(BF16), x, mod3, final_g)


def kernel(x, c, fox_norm_g, fox_w_ada, fox_b_ada, fox_w_in, fox_b_f, fox_w_out,
           sb_norm_g, sb_w_ada, sb_b_ada, sb_w_in, sb_w_out, final_norm_g):
    depth = fox_norm_g.shape[0] + sb_norm_g.shape[0]
    for i in range(depth):
        j = i // 2
        final_g = final_norm_g if i == depth - 1 else None
        if i % 2 == 0:
            x = _fox_layer(x, c, fox_norm_g[j], fox_w_ada[j], fox_b_ada[j],
                           fox_w_in[j], fox_b_f[j], fox_w_out[j], final_g)
        else:
            x = _sb_layer(x, c, sb_norm_g[j], sb_w_ada[j], sb_b_ada[j],
                          sb_w_in[j], sb_w_out[j], final_g)
    return x
```

```python
import math

import jax
import jax.numpy as jnp
from jax import lax
from jax.experimental import pallas as pl
from jax.experimental.pallas import tpu as pltpu

D_MODEL = 1024
D_INNER = 2048
HEAD_DIM = 64
N_HEADS = D_INNER // HEAD_DIM
NORM_EPS = 1e-6
LOG2E = math.log2(math.e)
QK_SCALE = HEAD_DIM ** -0.5 * LOG2E

LANES = 128
HEADS_PER_BLOCK = LANES // HEAD_DIM
VMEM_LIMIT = 56 * 1024 * 1024

ROW_TILE = 512
COL_CHUNK = 512
ADA_COLS = 512
TQ = 256
TK = 256
UNROLL = 4
FOX_STATIC_UNROLL = 12
SB_STATIC_UNROLL = 16
VROWS = 2 * HEAD_DIM
BIAS_PARTS = 3
BIAS_STRIDE = 8
MASKED_SCORE = -1e30
TRI_ROWS = TK + 16
FIRST_LIVE_TILE = 3
SKIP_DISTANCE = 3
DEAD_GAP = 140.0

F32 = jnp.float32
BF16 = jnp.bfloat16


def _split3(x):
    hi = x.astype(BF16)
    r1 = x - hi.astype(F32)
    mid = r1.astype(BF16)
    lo = (r1 - mid.astype(F32)).astype(BF16)
    return hi, mid, lo


def _log_sigmoid(x):
    return jnp.minimum(x, 0.0) - jnp.log1p(jnp.exp(-jnp.abs(x)))


def _adaln_kernel(c_ref, w_ref, b_ref, o_ref):
    c = c_ref[...]
    s = c * jax.nn.sigmoid(c)
    o_ref[...] = jnp.dot(s, w_ref[...], preferred_element_type=F32,
                         precision=lax.Precision.HIGHEST) + b_ref[...]


def _adaln(c, w_ada, b_ada):
    b, d = c.shape
    n = w_ada.shape[1]
    return pl.pallas_call(
        _adaln_kernel,
        grid=(n // ADA_COLS,),
        in_specs=[pl.BlockSpec((b, d), lambda j: (0, 0)),
                  pl.BlockSpec((d, ADA_COLS), lambda j: (0, j)),
                  pl.BlockSpec((1, ADA_COLS), lambda j: (0, j))],
        out_specs=pl.BlockSpec((b, ADA_COLS), lambda j: (0, j)),
        out_shape=jax.ShapeDtypeStruct((b, n), F32),
        name="adaln",
    )(c, w_ada, b_ada.reshape(1, n))


def _modulated_norm(x_ref, shift_ref, scale_ref, g_ref):
    x = x_ref[0]
    y = x * lax.rsqrt(jnp.mean(x * x, axis=-1, keepdims=True) + NORM_EPS) * g_ref[...]
    return (y * (1.0 + scale_ref[0]) + shift_ref[0]).astype(BF16)


def _project(h, wkz_ref, wqvt_ref, qt_ref, k_ref, vt_ref, z_ref):
    chunks = D_INNER // COL_CHUNK
    for o, out_ref in enumerate((k_ref, z_ref)):
        for cc in range(chunks):
            col = o * D_INNER + cc * COL_CHUNK
            out_ref[0, :, cc * COL_CHUNK:(cc + 1) * COL_CHUNK] = jnp.dot(
                h, wkz_ref[:, col:col + COL_CHUNK], preferred_element_type=F32).astype(BF16)
    for o, (out_ref, scale) in enumerate(((qt_ref, QK_SCALE), (vt_ref, None))):
        for cc in range(chunks):
            row = o * D_INNER + cc * COL_CHUNK
            t = lax.dot_general(wqvt_ref[row:row + COL_CHUNK, :], h, (((1,), (1,)), ((), ())),
                                preferred_element_type=F32)
            if scale is not None:
                t = t * scale
            out_ref[0, cc * COL_CHUNK:(cc + 1) * COL_CHUNK, :] = t.astype(BF16)


def _inproj_sb_kernel(x_ref, shift_ref, scale_ref, g_ref, wkz_ref, wqvt_ref,
                      qt_ref, k_ref, vt_ref, z_ref):
    h = _modulated_norm(x_ref, shift_ref, scale_ref, g_ref)
    _project(h, wkz_ref, wqvt_ref, qt_ref, k_ref, vt_ref, z_ref)


def _inproj_fox_kernel(x_ref, shift_ref, scale_ref, g_ref, wkz_ref, wqvt_ref, wf_ref, bf_ref,
                       qt_ref, k_ref, vt_ref, z_ref, hi_ref, mid_ref, lo_ref, carry_ref):
    h = _modulated_norm(x_ref, shift_ref, scale_ref, g_ref)
    _project(h, wkz_ref, wqvt_ref, qt_ref, k_ref, vt_ref, z_ref)

    @pl.when(pl.program_id(1) == 0)
    def _():
        carry_ref[...] = jnp.zeros_like(carry_ref)

    log_f = _log_sigmoid(jnp.dot(h, wf_ref[...], preferred_element_type=F32) + bf_ref[...])
    rows = log_f.shape[0]
    r = lax.broadcasted_iota(jnp.int32, (rows, rows), 0)
    c = lax.broadcasted_iota(jnp.int32, (rows, rows), 1)
    tri = (c <= r).astype(BF16)
    hi, mid, lo = _split3(log_f)
    cum = (jnp.dot(tri, hi, preferred_element_type=F32)
           + jnp.dot(tri, mid, preferred_element_type=F32)
           + jnp.dot(tri, lo, preferred_element_type=F32)) + carry_ref[...]
    carry_ref[...] = cum[rows - 1:rows, :]
    hi_ref[0], mid_ref[0], lo_ref[0] = _split3(cum * (-LOG2E))


def _inproj(x, mod3, g, wkz, wqvt, forget=None):
    b, s, d = x.shape
    row_spec = lambda width: pl.BlockSpec((1, ROW_TILE, width), lambda i, j: (i, j, 0))
    col_spec = pl.BlockSpec((1, D_INNER, ROW_TILE), lambda i, j: (i, 0, j))
    mod_spec = lambda k: pl.BlockSpec((1, 1, d), lambda i, j: (3 * i + k, 0, 0))
    const = lambda shape: pl.BlockSpec(shape, lambda i, j: (0,) * len(shape),
                                       pipeline_mode=pl.Buffered(1))
    in_specs = [row_spec(d), mod_spec(0), mod_spec(1), const((1, d)),
                const(wkz.shape), const(wqvt.shape)]
    args = [x, mod3, mod3, g.reshape(1, d), wkz, wqvt]
    row_major = jax.ShapeDtypeStruct((b, s, D_INNER), BF16)
    col_major = jax.ShapeDtypeStruct((b, D_INNER, s), BF16)
    out_specs = [col_spec, row_spec(D_INNER), col_spec, row_spec(D_INNER)]
    out_shape = [col_major, row_major, col_major, row_major]
    scratch = []
    kernel = _inproj_sb_kernel
    if forget is not None:
        wf, bf = forget
        in_specs += [const(wf.shape), const(bf.shape)]
        args += [wf, bf]
        out_specs += [row_spec(LANES)] * BIAS_PARTS
        out_shape += [jax.ShapeDtypeStruct((b, s, LANES), BF16)] * BIAS_PARTS
        scratch = [pltpu.VMEM((1, LANES), F32)]
        kernel = _inproj_fox_kernel
    return pl.pallas_call(
        kernel,
        grid=(b, s // ROW_TILE),
        in_specs=in_specs,
        out_specs=out_specs,
        out_shape=out_shape,
        scratch_shapes=scratch,
        compiler_params=pltpu.CompilerParams(
            dimension_semantics=("parallel", "arbitrary"), vmem_limit_bytes=VMEM_LIMIT),
        name="inproj_fox" if forget is not None else "inproj_sb",
    )(*args)


def _projection_weights(w_in):
    q, k, v, z = (w_in[:, i * D_INNER:(i + 1) * D_INNER] for i in range(4))
    return (jnp.concatenate([k, z], axis=1).astype(BF16),
            jnp.concatenate([q, v], axis=1).T.astype(BF16))


def _mask_keys_to_head(k, h):
    lane = lax.broadcasted_iota(jnp.int32, k.shape, 1)
    return jnp.where(lane // HEAD_DIM == h, k, jnp.zeros_like(k))


def _finalize(z_ref, o_ref, acc_ref, nq, normalize):
    for i in range(nq):
        outs = []
        for h in range(HEADS_PER_BLOCK):
            acc = acc_ref[i, h]
            o = acc[:HEAD_DIM]
            if normalize:
                o = o / acc[HEAD_DIM:HEAD_DIM + 1]
            outs.append(o)
        o = jnp.concatenate(outs, axis=0).T
        z = z_ref[0, i * TQ:(i + 1) * TQ, :].astype(F32)
        o_ref[0, i * TQ:(i + 1) * TQ, :] = (o * (z * jax.nn.sigmoid(z))).astype(o_ref.dtype)


def _wave_pair(n, nq):
    n = jnp.clip(n, 0, nq * (nq + 1) // 2 - 1)
    d = jnp.int32(0)
    for e in range(1, nq):
        d = d + (n >= e * nq - e * (e - 1) // 2).astype(jnp.int32)
    qi = n - (d * nq - d * (d - 1) // 2) + d
    return qi, qi - d


def _fox_kernel(qt_ref, k_ref, bp_ref, vt_ref, z_ref, o_ref,
                kx_ref, w_ref, vx_ref, mk_ref, s0_ref, s1_ref, p0_ref, p1_ref, m_ref, acc_ref, bnd_ref):
    s_len = k_ref.shape[1]
    nq = s_len // TQ
    n_pairs = nq * (nq + 1) // 2
    hp = pl.program_id(1)
    k = k_ref[0]
    bp = bp_ref[0]
    lane = lax.broadcasted_iota(jnp.int32, bp.shape, 1)
    for h in range(HEADS_PER_BLOCK):
        base = BIAS_STRIDE * hp + BIAS_PARTS * h
        kx_ref[h, :, :LANES] = _mask_keys_to_head(k, h)
        kx_ref[h, :, LANES:] = jnp.where((lane >= base) & (lane < base + BIAS_PARTS),
                                         bp, jnp.zeros_like(bp))
        vx_ref[h, :HEAD_DIM, :] = vt_ref[0, h * HEAD_DIM:(h + 1) * HEAD_DIM, :]
        vx_ref[h, HEAD_DIM:, :] = jnp.ones((VROWS - HEAD_DIM, s_len), BF16)
    w_ref[:LANES, :] = qt_ref[0]
    w_ref[LANES:, :] = jnp.ones((LANES, s_len), BF16)
    k_abs = jnp.max(jnp.abs(k.astype(F32)), axis=0, keepdims=True)
    lane1 = lax.broadcasted_iota(jnp.int32, (8, LANES), 1)
    k_abs = jnp.concatenate([jnp.where(lane1 // HEAD_DIM == h, k_abs, 0.0)
                             for h in range(HEADS_PER_BLOCK)], axis=0).astype(BF16)
    bound = jnp.dot(k_abs, jnp.abs(qt_ref[0]), preferred_element_type=F32)
    for h in range(HEADS_PER_BLOCK):
        bnd_ref[h] = bound[8 * h:8 * h + 1]
    key = lax.broadcasted_iota(jnp.int32, (TK, TQ), 0)
    qry = lax.broadcasted_iota(jnp.int32, (TK, TQ), 1)
    mk_ref[0] = jnp.zeros((TK, TQ), F32)
    mk_ref[1] = jnp.where(key <= qry, 0.0, MASKED_SCORE)
    m_ref[...] = jnp.full(m_ref.shape, MASKED_SCORE, F32)
    acc_ref[...] = jnp.zeros_like(acc_ref)
    p1_ref[...] = jnp.zeros_like(p1_ref)
    s_bufs, p_bufs = (s0_ref, s1_ref), (p0_ref, p1_ref)

    def score_stage(n, s_ref):
        qi, kj = _wave_pair(n, nq)
        w = w_ref[:, pl.ds(qi * TQ, TQ)]
        for h in range(HEADS_PER_BLOCK):
            s_ref[h] = jnp.dot(kx_ref[h, pl.ds(kj * TK, TK), :], w, preferred_element_type=F32)

    def softmax_stage(n, s_ref, p_ref, last):
        qi, kj = _wave_pair(n, nq)
        diagonal = (qi == kj).astype(jnp.int32)
        m_out, al_out = [], []
        for h in range(HEADS_PER_BLOCK):
            st = s_ref[h] + mk_ref[diagonal]
            m = jnp.where(qi == last[0], last[1][h], m_ref[qi, h])
            m_new = jnp.maximum(m, jnp.max(st, axis=0, keepdims=True))
            al_out.append(jnp.exp2(m - m_new))
            p_ref[h] = jnp.exp2(st - m_new).astype(BF16)
            m_ref[qi, h] = m_new
            m_out.append(m_new)
        return (qi, m_out), al_out

    def value_stage(n, p_ref, al):
        qi, kj = _wave_pair(n, nq)
        for h in range(HEADS_PER_BLOCK):
            pv = jnp.dot(vx_ref[h, :, pl.ds(kj * TK, TK)], p_ref[h], preferred_element_type=F32)
            acc_ref[qi, h] = al[h] * acc_ref[qi, h] + pv

    def steps(it, state, unroll=UNROLL):
        al_prv, last = state
        for u in range(unroll):
            n = it * unroll + u
            score_stage(n + 1, s_bufs[1 - u % 2])
            last, al = softmax_stage(n, s_bufs[u % 2], p_bufs[u % 2], last)
            value_stage(n - 1, p_bufs[1 - u % 2], al_prv)
            al_prv = al
        return al_prv, last

    score_stage(0, s_bufs[0])
    row = lambda value: [jnp.full((1, TQ), value, F32)] * HEADS_PER_BLOCK
    n_static = min(-(-(SKIP_DISTANCE * nq - SKIP_DISTANCE * (SKIP_DISTANCE - 1) // 2) // UNROLL),
                   n_pairs // UNROLL)
    assert (n_static * UNROLL) % FOX_STATIC_UNROLL == 0
    state = lax.fori_loop(0, n_static * UNROLL // FOX_STATIC_UNROLL,
                          lambda it, st: steps(it, st, FOX_STATIC_UNROLL), (row(1.0), (jnp.int32(-1), row(0.0))))

    def live(carry):
        it = carry[0]
        qi0, kj0 = _wave_pair(it * UNROLL, nq)
        nearest = qi0 - kj0
        worst = jnp.float32(-jnp.inf)
        for i in range(SKIP_DISTANCE, nq):
            last_rows = (jnp.maximum(i - nearest, 0) + 1) * TK - 16
            for h in range(HEADS_PER_BLOCK):
                bias_end = jnp.sum(kx_ref[h, pl.ds(last_rows, 16), LANES:][15:].astype(F32))
                gap = jnp.max(bnd_ref[h, :, i * TQ:(i + 1) * TQ] - m_ref[i, h])
                worst = jnp.maximum(worst, jnp.where(i >= nearest, bias_end + gap, -jnp.inf))
        return (it < n_pairs // UNROLL) & (worst > -DEAD_GAP)

    def body(carry):
        it, state = carry
        return it + 1, steps(it, state)

    it, (al_prv, _) = lax.while_loop(live, body, (jnp.int32(n_static), state))
    value_stage(it * UNROLL - 1, p_bufs[1], al_prv)
    _finalize(z_ref, o_ref, acc_ref, nq, True)


def _sb_kernel(qt_ref, k_ref, vt_ref, z_ref, tri_ref, o_ref,
               kx_ref, mk_ref, s0_ref, s1_ref, lk0_ref, lk1_ref, lb0_ref, lb1_ref,
               p0_ref, p1_ref, r_ref, acc_ref):
    s_len = k_ref.shape[1]
    nq = s_len // TQ
    n_pairs = nq * (nq + 1) // 2
    k = k_ref[0]
    for h in range(HEADS_PER_BLOCK):
        kx_ref[h] = _mask_keys_to_head(k, h)
    key = lax.broadcasted_iota(jnp.int32, (TK, TQ), 0)
    qry = lax.broadcasted_iota(jnp.int32, (TK, TQ), 1)
    mk_ref[0] = jnp.zeros((TK, TQ), F32)
    mk_ref[1] = jnp.where(key < qry, 0.0, MASKED_SCORE)
    r_ref[...] = jnp.zeros_like(r_ref)
    acc_ref[...] = jnp.zeros_like(acc_ref)
    p1_ref[...] = jnp.zeros_like(p1_ref)
    s_bufs, lk_bufs, lb_bufs, p_bufs = ((s0_ref, s1_ref), (lk0_ref, lk1_ref),
                                        (lb0_ref, lb1_ref), (p0_ref, p1_ref))

    pair = lambda n: _wave_pair(n, nq)

    def score_stage(n, s_ref):
        qi, kj = pair(n)
        w = qt_ref[0, :, pl.ds(qi * TQ, TQ)]
        for h in range(HEADS_PER_BLOCK):
            s_ref[h] = jnp.dot(kx_ref[h, pl.ds(kj * TK, TK), :], w, preferred_element_type=F32)

    def gate_stage(n, s_ref, lk_ref, lb_ref):
        qi, kj = pair(n)
        diagonal = (qi == kj).astype(jnp.int32)
        for h in range(HEADS_PER_BLOCK):
            zz = s_ref[h] + mk_ref[diagonal]
            neg_abs = lax.bitcast_convert_type(
                lax.bitcast_convert_type(zz, jnp.uint32) | jnp.uint32(0x80000000), F32)
            lb = jnp.minimum(zz, 0.0) - jnp.log2(1.0 + jnp.exp2(neg_abs))
            lk_ref[h] = (lb - zz).astype(BF16)
            lb_ref[h] = lb

    def weight_stage(lk_ref, lb_ref, p_ref):
        tot = []
        for h in range(HEADS_PER_BLOCK):
            after = jnp.dot(tri_ref[...], lk_ref[h], preferred_element_type=F32)
            p_ref[h] = jnp.exp2(lb_ref[h] + after[:TK]).astype(BF16)
            tot.append(after[TK:TK + 1])
        return tot

    def value_stage(n, p_ref, tot, last):
        qi, kj = pair(n)
        r_out = []
        for h in range(HEADS_PER_BLOCK):
            pv = jnp.dot(vt_ref[0, h * HEAD_DIM:(h + 1) * HEAD_DIM, pl.ds(kj * TK, TK)], p_ref[h],
                         preferred_element_type=F32)
            r = jnp.where(qi == last[0], last[1][h], r_ref[qi, h])
            acc_ref[qi, h] += jnp.exp2(r) * pv
            r_ref[qi, h] = r + tot[h]
            r_out.append(r + tot[h])
        return qi, r_out

    def steps(it, state, unroll=UNROLL):
        tot_prv, last = state
        for u in range(unroll):
            n = it * unroll + u
            score_stage(n + 2, s_bufs[u % 2])
            gate_stage(n + 1, s_bufs[1 - u % 2], lk_bufs[1 - u % 2], lb_bufs[1 - u % 2])
            tot = weight_stage(lk_bufs[u % 2], lb_bufs[u % 2], p_bufs[u % 2])
            last = value_stage(n - 1, p_bufs[1 - u % 2], tot_prv, last)
            tot_prv = tot
        return tot_prv, last

    score_stage(0, s_bufs[0])
    gate_stage(0, s_bufs[0], lk_bufs[0], lb_bufs[0])
    score_stage(1, s_bufs[1])

    n_static = min(-(-(2 * nq - 1) // UNROLL), n_pairs // UNROLL)
    zero_row = [jnp.zeros((1, TQ), F32)] * HEADS_PER_BLOCK
    assert (n_static * UNROLL) % SB_STATIC_UNROLL == 0
    state = lax.fori_loop(0, n_static * UNROLL // SB_STATIC_UNROLL,
                          lambda it, st: steps(it, st, SB_STATIC_UNROLL), (zero_row, (jnp.int32(-1), zero_row)))

    def live(carry):
        it = carry[0]
        if nq <= FIRST_LIVE_TILE:
            return it < 0
        return (it < n_pairs // UNROLL) & (jnp.max(jnp.exp2(r_ref[FIRST_LIVE_TILE:])) > 0.0)

    def body(carry):
        it, state = carry
        return it + 1, steps(it, state)

    it, (tot_prv, last) = lax.while_loop(live, body, (jnp.int32(n_static), state))
    value_stage(it * UNROLL - 1, p_bufs[1], tot_prv, last)
    _finalize(z_ref, o_ref, acc_ref, nq, False)


def _mixer_call(kernel, name, args, in_specs, scratch, b, s):
    seq_spec = pl.BlockSpec((1, s, LANES), lambda i, j: (i, 0, j))
    return pl.pallas_call(
        kernel,
        grid=(b, D_INNER // LANES),
        in_specs=in_specs,
        out_specs=seq_spec,
        out_shape=jax.ShapeDtypeStruct((b, s, D_INNER), BF16),
        scratch_shapes=scratch,
        compiler_params=pltpu.CompilerParams(
            dimension_semantics=("parallel", "parallel"), vmem_limit_bytes=VMEM_LIMIT),
        name=name,
    )(*args)


def _fox_mixer(qt, k, bp, vt, z):
    b, s, _ = k.shape
    nq = s // TQ
    assert (nq * (nq + 1) // 2) % UNROLL == 0
    seq_spec = pl.BlockSpec((1, s, LANES), lambda i, j: (i, 0, j))
    t_spec = pl.BlockSpec((1, LANES, s), lambda i, j: (i, j, 0))
    bp_spec = pl.BlockSpec((1, s, LANES), lambda i, j: (i, 0, 0))
    tile = (HEADS_PER_BLOCK, TK, TQ)
    scratch = [pltpu.VMEM((HEADS_PER_BLOCK, s, 2 * LANES), BF16),
               pltpu.VMEM((2 * LANES, s), BF16),
               pltpu.VMEM((HEADS_PER_BLOCK, VROWS, s), BF16),
               pltpu.VMEM((2, TK, TQ), F32),
               pltpu.VMEM(tile, F32), pltpu.VMEM(tile, F32),
               pltpu.VMEM(tile, BF16), pltpu.VMEM(tile, BF16),
               pltpu.VMEM((nq, HEADS_PER_BLOCK, 1, TQ), F32),
               pltpu.VMEM((nq, HEADS_PER_BLOCK, VROWS, TQ), F32),
               pltpu.VMEM((HEADS_PER_BLOCK, 1, s), F32)]
    return _mixer_call(_fox_kernel, "fox", (qt, k, bp, vt, z),
                       [t_spec, seq_spec, bp_spec, t_spec, seq_spec], scratch, b, s)


def _sb_mixer(qt, k, vt, z):
    b, s, _ = k.shape
    nq = s // TQ
    assert (nq * (nq + 1) // 2) % UNROLL == 0
    r = lax.broadcasted_iota(jnp.int32, (TRI_ROWS, TK), 0)
    c = lax.broadcasted_iota(jnp.int32, (TRI_ROWS, TK), 1)
    tri = ((c > r) | (r >= TK)).astype(BF16)
    seq_spec = pl.BlockSpec((1, s, LANES), lambda i, j: (i, 0, j))
    t_spec = pl.BlockSpec((1, LANES, s), lambda i, j: (i, j, 0))
    tri_spec = pl.BlockSpec((TRI_ROWS, TK), lambda i, j: (0, 0))
    tile = (HEADS_PER_BLOCK, TK, TQ)
    scratch = [pltpu.VMEM((HEADS_PER_BLOCK, s, LANES), BF16),
               pltpu.VMEM((2, TK, TQ), F32),
               pltpu.VMEM(tile, F32), pltpu.VMEM(tile, F32),
               pltpu.VMEM(tile, BF16), pltpu.VMEM(tile, BF16),
               pltpu.VMEM(tile, F32), pltpu.VMEM(tile, F32),
               pltpu.VMEM(tile, BF16), pltpu.VMEM(tile, BF16),
               pltpu.VMEM((nq, HEADS_PER_BLOCK, 1, TQ), F32),
               pltpu.VMEM((nq, HEADS_PER_BLOCK, HEAD_DIM, TQ), F32)]
    return _mixer_call(_sb_kernel, "sb", (qt, k, vt, z, tri),
                       [t_spec, seq_spec, t_spec, seq_spec, tri_spec], scratch, b, s)


def _outproj_kernel(u_ref, w_ref, x_ref, gate_ref, o_ref):
    y = jnp.dot(u_ref[0], w_ref[...], preferred_element_type=F32)
    o_ref[0] = x_ref[0] + gate_ref[0] * y


def _outproj_final_kernel(u_ref, w_ref, x_ref, gate_ref, g_ref, o_ref):
    y = jnp.dot(u_ref[0], w_ref[...], preferred_element_type=F32)
    r = x_ref[0] + gate_ref[0] * y
    o_ref[0] = r * lax.rsqrt(jnp.mean(r * r, axis=-1, keepdims=True) + NORM_EPS) * g_ref[...]


def _outproj(u, w, x, mod3, final_g=None):
    b, s, d = x.shape
    row_spec = lambda width: pl.BlockSpec((1, ROW_TILE, width), lambda i, j: (i, j, 0))
    const = lambda shape: pl.BlockSpec(shape, lambda i, j: (0,) * len(shape),
                                       pipeline_mode=pl.Buffered(1))
    in_specs = [row_spec(D_INNER), const(w.shape), row_spec(d),
                pl.BlockSpec((1, 1, d), lambda i, j: (3 * i + 2, 0, 0))]
    args = [u, w, x, mod3]
    kernel = _outproj_kernel
    if final_g is not None:
        in_specs.append(const((1, d)))
        args.append(final_g.reshape(1, d))
        kernel = _outproj_final_kernel
    return pl.pallas_call(
        kernel,
        grid=(b, s // ROW_TILE),
        in_specs=in_specs,
        out_specs=row_spec(d),
        out_shape=jax.ShapeDtypeStruct((b, s, d), F32),
        compiler_params=pltpu.CompilerParams(
            dimension_semantics=("parallel", "parallel"), vmem_limit_bytes=VMEM_LIMIT),
        name="outproj_final" if final_g is not None else "outproj",
    )(*args)


def _arrange_bias_parts(parts):
    b, s, _ = parts[0].shape
    x = jnp.stack([p[:, :, :N_HEADS] for p in parts], axis=-1)
    x = x.reshape(b, s, N_HEADS // HEADS_PER_BLOCK, HEADS_PER_BLOCK * BIAS_PARTS)
    x = jnp.pad(x, ((0, 0), (0, 0), (0, 0), (0, BIAS_STRIDE - HEADS_PER_BLOCK * BIAS_PARTS)))
    return x.reshape(b, s, LANES)


def _fox_layer(x, c, norm_g, w_ada, b_ada, w_in, b_f, w_out, final_g):
    b, s, d = x.shape
    mod3 = _adaln(c, w_ada, b_ada).reshape(b * 3, 1, d)
    wkz, wqvt = _projection_weights(w_in)
    wf = jnp.pad(w_in[:, 4 * D_INNER:], ((0, 0), (0, LANES - N_HEADS))).astype(BF16)
    bf = jnp.pad(b_f, (0, LANES - N_HEADS)).reshape(1, LANES)
    qt, k, vt, z, hi, mid, lo = _inproj(x, mod3, norm_g, wkz, wqvt, forget=(wf, bf))
    u = _fox_mixer(qt, k, _arrange_bias_parts((hi, mid, lo)), vt, z)
    return _outproj(u, w_out.astype(BF16), x, mod3, final_g)


def _sb_layer(x, c, norm_g, w_ada, b_ada, w_in, w_out, final_g):
    b, s, d = x.shape
    mod3 = _adaln(c, w_ada, b_ada).reshape(b * 3, 1, d)
    wkz, wqvt = _projection_weights(w_in)
    qt, k, vt, z = _inproj(x, mod3, norm_g, wkz, wqvt)
    u = _sb_mixer(qt, k, vt, z)
    return _outproj(u, w_out.astype(BF16), x, mod3, final_g)


def kernel(x, c, fox_norm_g, fox_w_ada, fox_b_ada, fox_w_in, fox_b_f, fox_w_out,
           sb_norm_g, sb_w_ada, sb_b_ada, sb_w_in, sb_w_out, final_norm_g):
    depth = fox_norm_g.shape[0] + sb_norm_g.shape[0]
    for i in range(depth):
        j = i // 2
        final_g = final_norm_g if i == depth - 1 else None
        if i % 2 == 0:
            x = _fox_layer(x, c, fox_norm_g[j], fox_w_ada[j], fox_b_ada[j],
                           fox_w_in[j], fox_b_f[j], fox_w_out[j], final_g)
        else:
            x = _sb_layer(x, c, sb_norm_g[j], sb_w_ada[j], sb_b_ada[j],
                          sb_w_in[j], sb_w_out[j], final_g)
    return x
```

```python
import math

import jax
import jax.numpy as jnp
from jax import lax
from jax.experimental import pallas as pl
from jax.experimental.pallas import tpu as pltpu

D_MODEL = 1024
D_INNER = 2048
HEAD_DIM = 64
N_HEADS = D_INNER // HEAD_DIM
NORM_EPS = 1e-6
LOG2E = math.log2(math.e)
QK_SCALE = HEAD_DIM ** -0.5 * LOG2E

LANES = 128
HEADS_PER_BLOCK = LANES // HEAD_DIM
VMEM_LIMIT = 56 * 1024 * 1024

ROW_TILE = 512
COL_CHUNK = 512
ADA_COLS = 512
TQ = 256
TK = 256
UNROLL = 4
VROWS = 2 * HEAD_DIM
BIAS_PARTS = 3
BIAS_STRIDE = 8
MASKED_SCORE = -1e30
TRI_ROWS = TK + 16
FIRST_LIVE_TILE = 3
SKIP_DISTANCE = 3
DEAD_GAP = 140.0

F32 = jnp.float32
BF16 = jnp.bfloat16


def _split3(x):
    hi = x.astype(BF16)
    r1 = x - hi.astype(F32)
    mid = r1.astype(BF16)
    lo = (r1 - mid.astype(F32)).astype(BF16)
    return hi, mid, lo


def _log_sigmoid(x):
    return jnp.minimum(x, 0.0) - jnp.log1p(jnp.exp(-jnp.abs(x)))


def _adaln_kernel(c_ref, w_ref, b_ref, o_ref):
    c = c_ref[...]
    s = c * jax.nn.sigmoid(c)
    o_ref[...] = jnp.dot(s, w_ref[...], preferred_element_type=F32,
                         precision=lax.Precision.HIGHEST) + b_ref[...]


def _adaln(c, w_ada, b_ada):
    b, d = c.shape
    n = w_ada.shape[1]
    return pl.pallas_call(
        _adaln_kernel,
        grid=(n // ADA_COLS,),
        in_specs=[pl.BlockSpec((b, d), lambda j: (0, 0)),
                  pl.BlockSpec((d, ADA_COLS), lambda j: (0, j)),
                  pl.BlockSpec((1, ADA_COLS), lambda j: (0, j))],
        out_specs=pl.BlockSpec((b, ADA_COLS), lambda j: (0, j)),
        out_shape=jax.ShapeDtypeStruct((b, n), F32),
        name="adaln",
    )(c, w_ada, b_ada.reshape(1, n))


def _modulated_norm(x_ref, shift_ref, scale_ref, g_ref):
    x = x_ref[0]
    y = x * lax.rsqrt(jnp.mean(x * x, axis=-1, keepdims=True) + NORM_EPS) * g_ref[...]
    return (y * (1.0 + scale_ref[0]) + shift_ref[0]).astype(BF16)


def _project(h, wkz_ref, wqvt_ref, qt_ref, k_ref, vt_ref, z_ref):
    chunks = D_INNER // COL_CHUNK
    for o, out_ref in enumerate((k_ref, z_ref)):
        for cc in range(chunks):
            col = o * D_INNER + cc * COL_CHUNK
            out_ref[0, :, cc * COL_CHUNK:(cc + 1) * COL_CHUNK] = jnp.dot(
                h, wkz_ref[:, col:col + COL_CHUNK], preferred_element_type=F32).astype(BF16)
    for o, (out_ref, scale) in enumerate(((qt_ref, QK_SCALE), (vt_ref, None))):
        for cc in range(chunks):
            row = o * D_INNER + cc * COL_CHUNK
            t = lax.dot_general(wqvt_ref[row:row + COL_CHUNK, :], h, (((1,), (1,)), ((), ())),
                                preferred_element_type=F32)
            if scale is not None:
                t = t * scale
            out_ref[0, cc * COL_CHUNK:(cc + 1) * COL_CHUNK, :] = t.astype(BF16)


def _inproj_sb_kernel(x_ref, shift_ref, scale_ref, g_ref, wkz_ref, wqvt_ref,
                      qt_ref, k_ref, vt_ref, z_ref):
    h = _modulated_norm(x_ref, shift_ref, scale_ref, g_ref)
    _project(h, wkz_ref, wqvt_ref, qt_ref, k_ref, vt_ref, z_ref)


def _inproj_fox_kernel(x_ref, shift_ref, scale_ref, g_ref, wkz_ref, wqvt_ref, wf_ref, bf_ref,
                       qt_ref, k_ref, vt_ref, z_ref, hi_ref, mid_ref, lo_ref, carry_ref):
    h = _modulated_norm(x_ref, shift_ref, scale_ref, g_ref)
    _project(h, wkz_ref, wqvt_ref, qt_ref, k_ref, vt_ref, z_ref)

    @pl.when(pl.program_id(1) == 0)
    def _():
        carry_ref[...] = jnp.zeros_like(carry_ref)

    log_f = _log_sigmoid(jnp.dot(h, wf_ref[...], preferred_element_type=F32) + bf_ref[...])
    rows = log_f.shape[0]
    r = lax.broadcasted_iota(jnp.int32, (rows, rows), 0)
    c = lax.broadcasted_iota(jnp.int32, (rows, rows), 1)
    tri = (c <= r).astype(BF16)
    hi, mid, lo = _split3(log_f)
    cum = (jnp.dot(tri, hi, preferred_element_type=F32)
           + jnp.dot(tri, mid, preferred_element_type=F32)
           + jnp.dot(tri, lo, preferred_element_type=F32)) + carry_ref[...]
    carry_ref[...] = cum[rows - 1:rows, :]
    hi_ref[0], mid_ref[0], lo_ref[0] = _split3(cum * (-LOG2E))


def _inproj(x, mod3, g, wkz, wqvt, forget=None):
    b, s, d = x.shape
    row_spec = lambda width: pl.BlockSpec((1, ROW_TILE, width), lambda i, j: (i, j, 0))
    col_spec = pl.BlockSpec((1, D_INNER, ROW_TILE), lambda i, j: (i, 0, j))
    mod_spec = lambda k: pl.BlockSpec((1, 1, d), lambda i, j: (3 * i + k, 0, 0))
    const = lambda shape: pl.BlockSpec(shape, lambda i, j: (0,) * len(shape),
                                       pipeline_mode=pl.Buffered(1))
    in_specs = [row_spec(d), mod_spec(0), mod_spec(1), const((1, d)),
                const(wkz.shape), const(wqvt.shape)]
    args = [x, mod3, mod3, g.reshape(1, d), wkz, wqvt]
    row_major = jax.ShapeDtypeStruct((b, s, D_INNER), BF16)
    col_major = jax.ShapeDtypeStruct((b, D_INNER, s), BF16)
    out_specs = [col_spec, row_spec(D_INNER), col_spec, row_spec(D_INNER)]
    out_shape = [col_major, row_major, col_major, row_major]
    scratch = []
    kernel = _inproj_sb_kernel
    if forget is not None:
        wf, bf = forget
        in_specs += [const(wf.shape), const(bf.shape)]
        args += [wf, bf]
        out_specs += [row_spec(LANES)] * BIAS_PARTS
        out_shape += [jax.ShapeDtypeStruct((b, s, LANES), BF16)] * BIAS_PARTS
        scratch = [pltpu.VMEM((1, LANES), F32)]
        kernel = _inproj_fox_kernel
    return pl.pallas_call(
        kernel,
        grid=(b, s // ROW_TILE),
        in_specs=in_specs,
        out_specs=out_specs,
        out_shape=out_shape,
        scratch_shapes=scratch,
        compiler_params=pltpu.CompilerParams(
            dimension_semantics=("parallel", "arbitrary"), vmem_limit_bytes=VMEM_LIMIT),
        name="inproj_fox" if forget is not None else "inproj_sb",
    )(*args)


def _projection_weights(w_in):
    q, k, v, z = (w_in[:, i * D_INNER:(i + 1) * D_INNER] for i in range(4))
    return (jnp.concatenate([k, z], axis=1).astype(BF16),
            jnp.concatenate([q, v], axis=1).T.astype(BF16))


def _mask_keys_to_head(k, h):
    lane = lax.broadcasted_iota(jnp.int32, k.shape, 1)
    return jnp.where(lane // HEAD_DIM == h, k, jnp.zeros_like(k))


def _finalize(z_ref, o_ref, acc_ref, nq, normalize):
    for i in range(nq):
        outs = []
        for h in range(HEADS_PER_BLOCK):
            acc = acc_ref[i, h]
            o = acc[:HEAD_DIM]
            if normalize:
                o = o / acc[HEAD_DIM:HEAD_DIM + 1]
            outs.append(o)
        o = jnp.concatenate(outs, axis=0).T
        z = z_ref[0, i * TQ:(i + 1) * TQ, :].astype(F32)
        o_ref[0, i * TQ:(i + 1) * TQ, :] = (o * (z * jax.nn.sigmoid(z))).astype(o_ref.dtype)


def _wave_pair(n, nq):
    n_pairs = nq * (nq + 1) // 2
    starts = [e * nq - e * (e - 1) // 2 for e in range(nq)]
    if isinstance(n, int):
        n = min(max(n, 0), n_pairs - 1)
        d = max(e for e in range(nq) if starts[e] <= n)
        return n - starts[d] + d, n - starts[d]
    n = jnp.clip(n, 0, n_pairs - 1)
    d = jnp.int32(0)
    for e in range(1, nq):
        d = d + (n >= starts[e]).astype(jnp.int32)
    qi = n - (d * nq - d * (d - 1) // 2) + d
    return qi, qi - d


def _pick(pred, a, b):
    if isinstance(pred, bool):
        return a if pred else b
    return jnp.where(pred, a, b)


def _fox_kernel(qt_ref, k_ref, bp_ref, vt_ref, z_ref, o_ref,
                kx_ref, w_ref, vx_ref, mk_ref, s0_ref, s1_ref, p0_ref, p1_ref, m_ref, acc_ref, bnd_ref):
    s_len = k_ref.shape[1]
    nq = s_len // TQ
    n_pairs = nq * (nq + 1) // 2
    hp = pl.program_id(1)
    k = k_ref[0]
    bp = bp_ref[0]
    lane = lax.broadcasted_iota(jnp.int32, bp.shape, 1)
    for h in range(HEADS_PER_BLOCK):
        base = BIAS_STRIDE * hp + BIAS_PARTS * h
        kx_ref[h, :, :LANES] = _mask_keys_to_head(k, h)
        kx_ref[h, :, LANES:] = jnp.where((lane >= base) & (lane < base + BIAS_PARTS),
                                         bp, jnp.zeros_like(bp))
        vx_ref[h, :HEAD_DIM, :] = vt_ref[0, h * HEAD_DIM:(h + 1) * HEAD_DIM, :]
        vx_ref[h, HEAD_DIM:, :] = jnp.ones((VROWS - HEAD_DIM, s_len), BF16)
    w_ref[:LANES, :] = qt_ref[0]
    w_ref[LANES:, :] = jnp.ones((LANES, s_len), BF16)
    k_abs = jnp.max(jnp.abs(k.astype(F32)), axis=0, keepdims=True)
    lane1 = lax.broadcasted_iota(jnp.int32, (8, LANES), 1)
    k_abs = jnp.concatenate([jnp.where(lane1 // HEAD_DIM == h, k_abs, 0.0)
                             for h in range(HEADS_PER_BLOCK)], axis=0).astype(BF16)
    bound = jnp.dot(k_abs, jnp.abs(qt_ref[0]), preferred_element_type=F32)
    for h in range(HEADS_PER_BLOCK):
        bnd_ref[h] = bound[8 * h:8 * h + 1]
    key = lax.broadcasted_iota(jnp.int32, (TK, TQ), 0)
    qry = lax.broadcasted_iota(jnp.int32, (TK, TQ), 1)
    mk_ref[0] = jnp.zeros((TK, TQ), F32)
    mk_ref[1] = jnp.where(key <= qry, 0.0, MASKED_SCORE)
    m_ref[...] = jnp.full(m_ref.shape, MASKED_SCORE, F32)
    acc_ref[...] = jnp.zeros_like(acc_ref)
    p1_ref[...] = jnp.zeros_like(p1_ref)
    s_bufs, p_bufs = (s0_ref, s1_ref), (p0_ref, p1_ref)

    def score_stage(n, s_ref):
        qi, kj = _wave_pair(n, nq)
        w = w_ref[:, pl.ds(qi * TQ, TQ)]
        for h in range(HEADS_PER_BLOCK):
            s_ref[h] = jnp.dot(kx_ref[h, pl.ds(kj * TK, TK), :], w, preferred_element_type=F32)

    def softmax_stage(n, s_ref, p_ref, last):
        qi, kj = _wave_pair(n, nq)
        diagonal = qi == kj
        m_out, al_out = [], []
        for h in range(HEADS_PER_BLOCK):
            if isinstance(diagonal, bool):
                st = s_ref[h] + mk_ref[1] if diagonal else s_ref[h]
            else:
                st = s_ref[h] + mk_ref[diagonal.astype(jnp.int32)]
            m = _pick(qi == last[0], last[1][h], m_ref[qi, h])
            m_new = jnp.maximum(m, jnp.max(st, axis=0, keepdims=True))
            al_out.append(jnp.exp2(m - m_new))
            p_ref[h] = jnp.exp2(st - m_new).astype(BF16)
            m_ref[qi, h] = m_new
            m_out.append(m_new)
        return (qi, m_out), al_out

    def value_stage(n, p_ref, al):
        qi, kj = _wave_pair(n, nq)
        for h in range(HEADS_PER_BLOCK):
            pv = jnp.dot(vx_ref[h, :, pl.ds(kj * TK, TK)], p_ref[h], preferred_element_type=F32)
            acc_ref[qi, h] = al[h] * acc_ref[qi, h] + pv

    def steps(it, state, unroll=UNROLL):
        al_prv, last = state
        for u in range(unroll):
            n = it * unroll + u
            score_stage(n + 1, s_bufs[1 - u % 2])
            last, al = softmax_stage(n, s_bufs[u % 2], p_bufs[u % 2], last)
            value_stage(n - 1, p_bufs[1 - u % 2], al_prv)
            al_prv = al
        return al_prv, last

    score_stage(0, s_bufs[0])
    row = lambda value: [jnp.full((1, TQ), value, F32)] * HEADS_PER_BLOCK
    n_static = min(-(-(SKIP_DISTANCE * nq - SKIP_DISTANCE * (SKIP_DISTANCE - 1) // 2) // UNROLL),
                   n_pairs // UNROLL)
    al_prv, last = steps(0, (row(1.0), (-1, row(0.0))), n_static * UNROLL)
    state = (al_prv, (jnp.int32(last[0]), last[1]))

    def live(carry):
        it = carry[0]
        qi0, kj0 = _wave_pair(it * UNROLL, nq)
        nearest = qi0 - kj0
        worst = jnp.float32(-jnp.inf)
        for i in range(SKIP_DISTANCE, nq):
            last_rows = (jnp.maximum(i - nearest, 0) + 1) * TK - 16
            for h in range(HEADS_PER_BLOCK):
                bias_end = jnp.sum(kx_ref[h, pl.ds(last_rows, 16), LANES:][15:].astype(F32))
                gap = jnp.max(bnd_ref[h, :, i * TQ:(i + 1) * TQ] - m_ref[i, h])
                worst = jnp.maximum(worst, jnp.where(i >= nearest, bias_end + gap, -jnp.inf))
        return (it < n_pairs // UNROLL) & (worst > -DEAD_GAP)

    def body(carry):
        it, state = carry
        return it + 1, steps(it, state)

    it, (al_prv, _) = lax.while_loop(live, body, (jnp.int32(n_static), state))
    value_stage(it * UNROLL - 1, p_bufs[1], al_prv)
    _finalize(z_ref, o_ref, acc_ref, nq, True)


def _sb_kernel(qt_ref, k_ref, vt_ref, z_ref, tri_ref, o_ref,
               kx_ref, mk_ref, s0_ref, s1_ref, lk0_ref, lk1_ref, lb0_ref, lb1_ref,
               p0_ref, p1_ref, r_ref, acc_ref):
    s_len = k_ref.shape[1]
    nq = s_len // TQ
    n_pairs = nq * (nq + 1) // 2
    k = k_ref[0]
    for h in range(HEADS_PER_BLOCK):
        kx_ref[h] = _mask_keys_to_head(k, h)
    key = lax.broadcasted_iota(jnp.int32, (TK, TQ), 0)
    qry = lax.broadcasted_iota(jnp.int32, (TK, TQ), 1)
    mk_ref[0] = jnp.zeros((TK, TQ), F32)
    mk_ref[1] = jnp.where(key < qry, 0.0, MASKED_SCORE)
    r_ref[...] = jnp.zeros_like(r_ref)
    acc_ref[...] = jnp.zeros_like(acc_ref)
    p1_ref[...] = jnp.zeros_like(p1_ref)
    s_bufs, lk_bufs, lb_bufs, p_bufs = ((s0_ref, s1_ref), (lk0_ref, lk1_ref),
                                        (lb0_ref, lb1_ref), (p0_ref, p1_ref))

    pair = lambda n: _wave_pair(n, nq)

    def score_stage(n, s_ref):
        qi, kj = pair(n)
        w = qt_ref[0, :, pl.ds(qi * TQ, TQ)]
        for h in range(HEADS_PER_BLOCK):
            s_ref[h] = jnp.dot(kx_ref[h, pl.ds(kj * TK, TK), :], w, preferred_element_type=F32)

    def gates(zz):
        neg_abs = lax.bitcast_convert_type(
            lax.bitcast_convert_type(zz, jnp.uint32) | jnp.uint32(0x80000000), F32)
        lb = jnp.minimum(zz, 0.0) - jnp.log2(1.0 + jnp.exp2(neg_abs))
        return lb, (lb - zz).astype(BF16)

    half = TK // 2
    live_blocks = ((slice(0, half), slice(0, TQ)), (slice(half, TK), slice(half, TQ)))

    def gate_stage(n, s_ref, lk_ref, lb_ref):
        qi, kj = pair(n)
        diagonal = qi == kj
        for h in range(HEADS_PER_BLOCK):
            if diagonal is True:
                for rows, cols in live_blocks:
                    lb, lk = gates(s_ref[h, rows, cols] + mk_ref[1, rows, cols])
                    lb_ref[h, rows, cols] = lb
                    lk_ref[h, rows, cols] = lk
                lk_ref[h, half:, :half] = jnp.zeros((TK - half, half), BF16)
            else:
                if diagonal is False:
                    zz = s_ref[h]
                else:
                    zz = s_ref[h] + mk_ref[diagonal.astype(jnp.int32)]
                lb_ref[h], lk_ref[h] = gates(zz)

    def weight_stage(n, lk_ref, lb_ref, p_ref):
        qi, kj = pair(n)
        tot = []
        for h in range(HEADS_PER_BLOCK):
            after = jnp.dot(tri_ref[...], lk_ref[h], preferred_element_type=F32)
            if (qi == kj) is True:
                for rows, cols in live_blocks:
                    p_ref[h, rows, cols] = jnp.exp2(lb_ref[h, rows, cols] + after[rows, cols]).astype(BF16)
                p_ref[h, half:, :half] = jnp.zeros((TK - half, half), BF16)
            else:
                p_ref[h] = jnp.exp2(lb_ref[h] + after[:TK]).astype(BF16)
            tot.append(after[TK:TK + 1])
        return tot

    def value_stage(n, p_ref, tot, last):
        qi, kj = pair(n)
        r_out = []
        for h in range(HEADS_PER_BLOCK):
            pv = jnp.dot(vt_ref[0, h * HEAD_DIM:(h + 1) * HEAD_DIM, pl.ds(kj * TK, TK)], p_ref[h],
                         preferred_element_type=F32)
            r = _pick(qi == last[0], last[1][h], r_ref[qi, h])
            acc_ref[qi, h] += jnp.exp2(r) * pv
            r_ref[qi, h] = r + tot[h]
            r_out.append(r + tot[h])
        return qi, r_out

    def steps(it, state, unroll=UNROLL):
        tot_prv, last = state
        for u in range(unroll):
            n = it * unroll + u
            score_stage(n + 2, s_bufs[u % 2])
            gate_stage(n + 1, s_bufs[1 - u % 2], lk_bufs[1 - u % 2], lb_bufs[1 - u % 2])
            tot = weight_stage(n, lk_bufs[u % 2], lb_bufs[u % 2], p_bufs[u % 2])
            last = value_stage(n - 1, p_bufs[1 - u % 2], tot_prv, last)
            tot_prv = tot
        return tot_prv, last

    score_stage(0, s_bufs[0])
    gate_stage(0, s_bufs[0], lk_bufs[0], lb_bufs[0])
    score_stage(1, s_bufs[1])

    n_static = min(-(-(2 * nq - 1) // UNROLL), n_pairs // UNROLL)
    zero_row = [jnp.zeros((1, TQ), F32)] * HEADS_PER_BLOCK
    tot_prv, last = steps(0, (zero_row, (-1, zero_row)), n_static * UNROLL)
    state = (tot_prv, (jnp.int32(last[0]), last[1]))

    def live(carry):
        it = carry[0]
        if nq <= FIRST_LIVE_TILE:
            return it < 0
        return (it < n_pairs // UNROLL) & (jnp.max(jnp.exp2(r_ref[FIRST_LIVE_TILE:])) > 0.0)

    def body(carry):
        it, state = carry
        return it + 1, steps(it, state)

    it, (tot_prv, last) = lax.while_loop(live, body, (jnp.int32(n_static), state))
    value_stage(it * UNROLL - 1, p_bufs[1], tot_prv, last)
    _finalize(z_ref, o_ref, acc_ref, nq, False)


def _mixer_call(kernel, name, args, in_specs, scratch, b, s):
    seq_spec = pl.BlockSpec((1, s, LANES), lambda i, j: (i, 0, j))
    return pl.pallas_call(
        kernel,
        grid=(b, D_INNER // LANES),
        in_specs=in_specs,
        out_specs=seq_spec,
        out_shape=jax.ShapeDtypeStruct((b, s, D_INNER), BF16),
        scratch_shapes=scratch,
        compiler_params=pltpu.CompilerParams(
            dimension_semantics=("parallel", "parallel"), vmem_limit_bytes=VMEM_LIMIT),
        name=name,
    )(*args)


def _fox_mixer(qt, k, bp, vt, z):
    b, s, _ = k.shape
    nq = s // TQ
    assert (nq * (nq + 1) // 2) % UNROLL == 0
    seq_spec = pl.BlockSpec((1, s, LANES), lambda i, j: (i, 0, j))
    t_spec = pl.BlockSpec((1, LANES, s), lambda i, j: (i, j, 0))
    bp_spec = pl.BlockSpec((1, s, LANES), lambda i, j: (i, 0, 0))
    tile = (HEADS_PER_BLOCK, TK, TQ)
    scratch = [pltpu.VMEM((HEADS_PER_BLOCK, s, 2 * LANES), BF16),
               pltpu.VMEM((2 * LANES, s), BF16),
               pltpu.VMEM((HEADS_PER_BLOCK, VROWS, s), BF16),
               pltpu.VMEM((2, TK, TQ), F32),
               pltpu.VMEM(tile, F32), pltpu.VMEM(tile, F32),
               pltpu.VMEM(tile, BF16), pltpu.VMEM(tile, BF16),
               pltpu.VMEM((nq, HEADS_PER_BLOCK, 1, TQ), F32),
               pltpu.VMEM((nq, HEADS_PER_BLOCK, VROWS, TQ), F32),
               pltpu.VMEM((HEADS_PER_BLOCK, 1, s), F32)]
    return _mixer_call(_fox_kernel, "fox", (qt, k, bp, vt, z),
                       [t_spec, seq_spec, bp_spec, t_spec, seq_spec], scratch, b, s)


def _sb_mixer(qt, k, vt, z):
    b, s, _ = k.shape
    nq = s // TQ
    assert (nq * (nq + 1) // 2) % UNROLL == 0
    r = lax.broadcasted_iota(jnp.int32, (TRI_ROWS, TK), 0)
    c = lax.broadcasted_iota(jnp.int32, (TRI_ROWS, TK), 1)
    tri = ((c > r) | (r >= TK)).astype(BF16)
    seq_spec = pl.BlockSpec((1, s, LANES), lambda i, j: (i, 0, j))
    t_spec = pl.BlockSpec((1, LANES, s), lambda i, j: (i, j, 0))
    tri_spec = pl.BlockSpec((TRI_ROWS, TK), lambda i, j: (0, 0))
    tile = (HEADS_PER_BLOCK, TK, TQ)
    scratch = [pltpu.VMEM((HEADS_PER_BLOCK, s, LANES), BF16),
               pltpu.VMEM((2, TK, TQ), F32),
               pltpu.VMEM(tile, F32), pltpu.VMEM(tile, F32),
               pltpu.VMEM(tile, BF16), pltpu.VMEM(tile, BF16),
               pltpu.VMEM(tile, F32), pltpu.VMEM(tile, F32),
               pltpu.VMEM(tile, BF16), pltpu.VMEM(tile, BF16),
               pltpu.VMEM((nq, HEADS_PER_BLOCK, 1, TQ), F32),
               pltpu.VMEM((nq, HEADS_PER_BLOCK, HEAD_DIM, TQ), F32)]
    return _mixer_call(_sb_kernel, "sb", (qt, k, vt, z, tri),
                       [t_spec, seq_spec, t_spec, seq_spec, tri_spec], scratch, b, s)


def _outproj_kernel(u_ref, w_ref, x_ref, gate_ref, o_ref):
    y = jnp.dot(u_ref[0], w_ref[...], preferred_element_type=F32)
    o_ref[0] = x_ref[0] + gate_ref[0] * y


def _outproj_final_kernel(u_ref, w_ref, x_ref, gate_ref, g_ref, o_ref):
    y = jnp.dot(u_ref[0], w_ref[...], preferred_element_type=F32)
    r = x_ref[0] + gate_ref[0] * y
    o_ref[0] = r * lax.rsqrt(jnp.mean(r * r, axis=-1, keepdims=True) + NORM_EPS) * g_ref[...]


def _outproj(u, w, x, mod3, final_g=None):
    b, s, d = x.shape
    row_spec = lambda width: pl.BlockSpec((1, ROW_TILE, width), lambda i, j: (i, j, 0))
    const = lambda shape: pl.BlockSpec(shape, lambda i, j: (0,) * len(shape),
                                       pipeline_mode=pl.Buffered(1))
    in_specs = [row_spec(D_INNER), const(w.shape), row_spec(d),
                pl.BlockSpec((1, 1, d), lambda i, j: (3 * i + 2, 0, 0))]
    args = [u, w, x, mod3]
    kernel = _outproj_kernel
    if final_g is not None:
        in_specs.append(const((1, d)))
        args.append(final_g.reshape(1, d))
        kernel = _outproj_final_kernel
    return pl.pallas_call(
        kernel,
        grid=(b, s // ROW_TILE),
        in_specs=in_specs,
        out_specs=row_spec(d),
        out_shape=jax.ShapeDtypeStruct((b, s, d), F32),
        compiler_params=pltpu.CompilerParams(
            dimension_semantics=("parallel", "parallel"), vmem_limit_bytes=VMEM_LIMIT),
        name="outproj_final" if final_g is not None else "outproj",
    )(*args)


def _arrange_bias_parts(parts):
    b, s, _ = parts[0].shape
    x = jnp.stack([p[:, :, :N_HEADS] for p in parts], axis=-1)
    x = x.reshape(b, s, N_HEADS // HEADS_PER_BLOCK, HEADS_PER_BLOCK * BIAS_PARTS)
    x = jnp.pad(x, ((0, 0), (0, 0), (0, 0), (0, BIAS_STRIDE - HEADS_PER_BLOCK * BIAS_PARTS)))
    return x.reshape(b, s, LANES)


def _fox_layer(x, c, norm_g, w_ada, b_ada, w_in, b_f, w_out, final_g):
    b, s, d = x.shape
    mod3 = _adaln(c, w_ada, b_ada).reshape(b * 3, 1, d)
    wkz, wqvt = _projection_weights(w_in)
    wf = jnp.pad(w_in[:, 4 * D_INNER:], ((0, 0), (0, LANES - N_HEADS))).astype(BF16)
    bf = jnp.pad(b_f, (0, LANES - N_HEADS)).reshape(1, LANES)
    qt, k, vt, z, hi, mid, lo = _inproj(x, mod3, norm_g, wkz, wqvt, forget=(wf, bf))
    u = _fox_mixer(qt, k, _arrange_bias_parts((hi, mid, lo)), vt, z)
    return _outproj(u, w_out.astype(BF16), x, mod3, final_g)


def _sb_layer(x, c, norm_g, w_ada, b_ada, w_in, w_out, final_g):
    b, s, d = x.shape
    mod3 = _adaln(c, w_ada, b_ada).reshape(b * 3, 1, d)
    wkz, wqvt = _projection_weights(w_in)
    qt, k, vt, z = _inproj(x, mod3, norm_g, wkz, wqvt)
    u = _sb_mixer(qt, k, vt, z)
    return _outproj(u, w_out.astype(BF16), x, mod3, final_g)


def kernel(x, c, fox_norm_g, fox_w_ada, fox_b_ada, fox_w_in, fox_b_f, fox_w_out,
           sb_norm_g, sb_w_ada, sb_b_ada, sb_w_in, sb_w_out, final_norm_g):
    depth = fox_norm_g.shape[0] + sb_norm_g.shape[0]
    for i in range(depth):
        j = i // 2
        final_g = final_norm_g if i == depth - 1 else None
        if i % 2 == 0:
            x = _fox_layer(x, c, fox_norm_g[j], fox_w_ada[j], fox_b_ada[j],
                           fox_w_in[j], fox_b_f[j], fox_w_out[j], final_g)
        else:
            x = _sb_layer(x, c, sb_norm_g[j], sb_w_ada[j], sb_b_ada[j],
                          sb_w_in[j], sb_w_out[j], final_g)
    return x
```

```python
import math

import jax
import jax.numpy as jnp
from jax import lax
from jax.experimental import pallas as pl
from jax.experimental.pallas import tpu as pltpu

D_MODEL = 1024
D_INNER = 2048
HEAD_DIM = 64
N_HEADS = D_INNER // HEAD_DIM
NORM_EPS = 1e-6
LOG2E = math.log2(math.e)
QK_SCALE = HEAD_DIM ** -0.5 * LOG2E

LANES = 128
HEADS_PER_BLOCK = LANES // HEAD_DIM
VMEM_LIMIT = 56 * 1024 * 1024

ROW_TILE = 512
COL_CHUNK = 512
ADA_COLS = 512
TQ = 256
TK = 256
UNROLL = 4
VROWS = 2 * HEAD_DIM
BIAS_PARTS = 3
BIAS_STRIDE = 8
MASKED_SCORE = -1e30
TRI_ROWS = TK + 16
FIRST_LIVE_TILE = 3
SKIP_DISTANCE = 3
DEAD_GAP = 140.0

F32 = jnp.float32
BF16 = jnp.bfloat16


def _split3(x):
    hi = x.astype(BF16)
    r1 = x - hi.astype(F32)
    mid = r1.astype(BF16)
    lo = (r1 - mid.astype(F32)).astype(BF16)
    return hi, mid, lo


def _log_sigmoid(x):
    return jnp.minimum(x, 0.0) - jnp.log1p(jnp.exp(-jnp.abs(x)))


def _adaln_kernel(c_ref, w_ref, b_ref, o_ref):
    c = c_ref[...]
    s = c * jax.nn.sigmoid(c)
    o_ref[...] = jnp.dot(s, w_ref[...], preferred_element_type=F32,
                         precision=lax.Precision.HIGHEST) + b_ref[...]


def _adaln(c, w_ada, b_ada):
    b, d = c.shape
    n = w_ada.shape[1]
    return pl.pallas_call(
        _adaln_kernel,
        grid=(n // ADA_COLS,),
        in_specs=[pl.BlockSpec((b, d), lambda j: (0, 0)),
                  pl.BlockSpec((d, ADA_COLS), lambda j: (0, j)),
                  pl.BlockSpec((1, ADA_COLS), lambda j: (0, j))],
        out_specs=pl.BlockSpec((b, ADA_COLS), lambda j: (0, j)),
        out_shape=jax.ShapeDtypeStruct((b, n), F32),
        name="adaln",
    )(c, w_ada, b_ada.reshape(1, n))


def _modulated_norm(x_ref, shift_ref, scale_ref, g_ref):
    x = x_ref[0]
    y = x * lax.rsqrt(jnp.mean(x * x, axis=-1, keepdims=True) + NORM_EPS) * g_ref[...]
    return (y * (1.0 + scale_ref[0]) + shift_ref[0]).astype(BF16)


def _project(h, wkz_ref, wqvt_ref, qt_ref, k_ref, vt_ref, z_ref):
    chunks = D_INNER // COL_CHUNK
    for o, out_ref in enumerate((k_ref, z_ref)):
        for cc in range(chunks):
            col = o * D_INNER + cc * COL_CHUNK
            out_ref[0, :, cc * COL_CHUNK:(cc + 1) * COL_CHUNK] = jnp.dot(
                h, wkz_ref[:, col:col + COL_CHUNK], preferred_element_type=F32).astype(BF16)
    for o, (out_ref, scale) in enumerate(((qt_ref, QK_SCALE), (vt_ref, None))):
        for cc in range(chunks):
            row = o * D_INNER + cc * COL_CHUNK
            t = lax.dot_general(wqvt_ref[row:row + COL_CHUNK, :], h, (((1,), (1,)), ((), ())),
                                preferred_element_type=F32)
            if scale is not None:
                t = t * scale
            out_ref[0, cc * COL_CHUNK:(cc + 1) * COL_CHUNK, :] = t.astype(BF16)


def _inproj_sb_kernel(x_ref, shift_ref, scale_ref, g_ref, wkz_ref, wqvt_ref,
                      qt_ref, k_ref, vt_ref, z_ref):
    h = _modulated_norm(x_ref, shift_ref, scale_ref, g_ref)
    _project(h, wkz_ref, wqvt_ref, qt_ref, k_ref, vt_ref, z_ref)


def _inproj_fox_kernel(x_ref, shift_ref, scale_ref, g_ref, wkz_ref, wqvt_ref, wf_ref, bf_ref,
                       qt_ref, k_ref, vt_ref, z_ref, hi_ref, mid_ref, lo_ref, carry_ref):
    h = _modulated_norm(x_ref, shift_ref, scale_ref, g_ref)
    _project(h, wkz_ref, wqvt_ref, qt_ref, k_ref, vt_ref, z_ref)

    @pl.when(pl.program_id(1) == 0)
    def _():
        carry_ref[...] = jnp.zeros_like(carry_ref)

    log_f = _log_sigmoid(jnp.dot(h, wf_ref[...], preferred_element_type=F32) + bf_ref[...])
    rows = log_f.shape[0]
    r = lax.broadcasted_iota(jnp.int32, (rows, rows), 0)
    c = lax.broadcasted_iota(jnp.int32, (rows, rows), 1)
    tri = (c <= r).astype(BF16)
    hi, mid, lo = _split3(log_f)
    cum = (jnp.dot(tri, hi, preferred_element_type=F32)
           + jnp.dot(tri, mid, preferred_element_type=F32)
           + jnp.dot(tri, lo, preferred_element_type=F32)) + carry_ref[...]
    carry_ref[...] = cum[rows - 1:rows, :]
    hi_ref[0], mid_ref[0], lo_ref[0] = _split3(cum * (-LOG2E))


def _inproj(x, mod3, g, wkz, wqvt, forget=None):
    b, s, d = x.shape
    row_spec = lambda width: pl.BlockSpec((1, ROW_TILE, width), lambda i, j: (i, j, 0))
    col_spec = pl.BlockSpec((1, D_INNER, ROW_TILE), lambda i, j: (i, 0, j))
    mod_spec = lambda k: pl.BlockSpec((1, 1, d), lambda i, j: (3 * i + k, 0, 0))
    const = lambda shape: pl.BlockSpec(shape, lambda i, j: (0,) * len(shape),
                                       pipeline_mode=pl.Buffered(1))
    in_specs = [row_spec(d), mod_spec(0), mod_spec(1), const((1, d)),
                const(wkz.shape), const(wqvt.shape)]
    args = [x, mod3, mod3, g.reshape(1, d), wkz, wqvt]
    row_major = jax.ShapeDtypeStruct((b, s, D_INNER), BF16)
    col_major = jax.ShapeDtypeStruct((b, D_INNER, s), BF16)
    out_specs = [col_spec, row_spec(D_INNER), col_spec, row_spec(D_INNER)]
    out_shape = [col_major, row_major, col_major, row_major]
    scratch = []
    kernel = _inproj_sb_kernel
    if forget is not None:
        wf, bf = forget
        in_specs += [const(wf.shape), const(bf.shape)]
        args += [wf, bf]
        out_specs += [row_spec(LANES)] * BIAS_PARTS
        out_shape += [jax.ShapeDtypeStruct((b, s, LANES), BF16)] * BIAS_PARTS
        scratch = [pltpu.VMEM((1, LANES), F32)]
        kernel = _inproj_fox_kernel
    return pl.pallas_call(
        kernel,
        grid=(b, s // ROW_TILE),
        in_specs=in_specs,
        out_specs=out_specs,
        out_shape=out_shape,
        scratch_shapes=scratch,
        compiler_params=pltpu.CompilerParams(
            dimension_semantics=("parallel", "arbitrary"), vmem_limit_bytes=VMEM_LIMIT),
        name="inproj_fox" if forget is not None else "inproj_sb",
    )(*args)


def _projection_weights(w_in):
    q, k, v, z = (w_in[:, i * D_INNER:(i + 1) * D_INNER] for i in range(4))
    return (jnp.concatenate([k, z], axis=1).astype(BF16),
            jnp.concatenate([q, v], axis=1).T.astype(BF16))


def _mask_keys_to_head(k, h):
    lane = lax.broadcasted_iota(jnp.int32, k.shape, 1)
    return jnp.where(lane // HEAD_DIM == h, k, jnp.zeros_like(k))


def _finalize(z_ref, o_ref, acc_ref, nq, normalize):
    for i in range(nq):
        outs = []
        for h in range(HEADS_PER_BLOCK):
            acc = acc_ref[i, h]
            o = acc[:HEAD_DIM]
            if normalize:
                o = o / acc[HEAD_DIM:HEAD_DIM + 1]
            outs.append(o)
        o = jnp.concatenate(outs, axis=0).T
        z = z_ref[0, i * TQ:(i + 1) * TQ, :].astype(F32)
        o_ref[0, i * TQ:(i + 1) * TQ, :] = (o * (z * jax.nn.sigmoid(z))).astype(o_ref.dtype)


def _wave_pair(n, nq):
    n_pairs = nq * (nq + 1) // 2
    starts = [e * nq - e * (e - 1) // 2 for e in range(nq)]
    if isinstance(n, int):
        n = min(max(n, 0), n_pairs - 1)
        d = max(e for e in range(nq) if starts[e] <= n)
        return n - starts[d] + d, n - starts[d]
    n = jnp.clip(n, 0, n_pairs - 1)
    d = jnp.int32(0)
    for e in range(1, nq):
        d = d + (n >= starts[e]).astype(jnp.int32)
    qi = n - (d * nq - d * (d - 1) // 2) + d
    return qi, qi - d


def _pick(pred, a, b):
    if isinstance(pred, bool):
        return a if pred else b
    return jnp.where(pred, a, b)


def _fox_kernel(qt_ref, k_ref, bp_ref, vt_ref, z_ref, o_ref,
                kx_ref, w_ref, vx_ref, mk_ref, s0_ref, s1_ref, p0_ref, p1_ref, m_ref, acc_ref, bnd_ref):
    s_len = k_ref.shape[1]
    nq = s_len // TQ
    n_pairs = nq * (nq + 1) // 2
    hp = pl.program_id(1)
    k = k_ref[0]
    bp = bp_ref[0]
    lane = lax.broadcasted_iota(jnp.int32, bp.shape, 1)
    for h in range(HEADS_PER_BLOCK):
        base = BIAS_STRIDE * hp + BIAS_PARTS * h
        kx_ref[h, :, :LANES] = _mask_keys_to_head(k, h)
        kx_ref[h, :, LANES:] = jnp.where((lane >= base) & (lane < base + BIAS_PARTS),
                                         bp, jnp.zeros_like(bp))
        vx_ref[h, :HEAD_DIM, :] = vt_ref[0, h * HEAD_DIM:(h + 1) * HEAD_DIM, :]
        vx_ref[h, HEAD_DIM:, :] = jnp.ones((VROWS - HEAD_DIM, s_len), BF16)
    w_ref[:LANES, :] = qt_ref[0]
    w_ref[LANES:, :] = jnp.ones((LANES, s_len), BF16)
    k_abs = jnp.max(jnp.abs(k.astype(F32)), axis=0, keepdims=True)
    lane1 = lax.broadcasted_iota(jnp.int32, (8, LANES), 1)
    k_abs = jnp.concatenate([jnp.where(lane1 // HEAD_DIM == h, k_abs, 0.0)
                             for h in range(HEADS_PER_BLOCK)], axis=0).astype(BF16)
    bound = jnp.dot(k_abs, jnp.abs(qt_ref[0]), preferred_element_type=F32)
    for h in range(HEADS_PER_BLOCK):
        bnd_ref[h] = bound[8 * h:8 * h + 1]
    key = lax.broadcasted_iota(jnp.int32, (TK, TQ), 0)
    qry = lax.broadcasted_iota(jnp.int32, (TK, TQ), 1)
    mk_ref[...] = jnp.where(key <= qry, 0.0, MASKED_SCORE)
    s_bufs, p_bufs = (s0_ref, s1_ref), (p0_ref, p1_ref)

    def score_stage(n, s_ref):
        qi, kj = _wave_pair(n, nq)
        w = w_ref[:, pl.ds(qi * TQ, TQ)]
        for h in range(HEADS_PER_BLOCK):
            s_ref[h] = jnp.dot(kx_ref[h, pl.ds(kj * TK, TK), :], w, preferred_element_type=F32)

    def softmax_stage(n, s_ref, p_ref, last):
        qi, kj = _wave_pair(n, nq)
        first = qi == kj
        assert isinstance(first, bool) or n_static_steps >= nq
        m_out, al_out = [], []
        for h in range(HEADS_PER_BLOCK):
            if first is True:
                st = s_ref[h] + mk_ref[...]
                m_new = jnp.max(st, axis=0, keepdims=True)
                al_out.append(None)
            else:
                st = s_ref[h]
                m = _pick(qi == last[0], last[1][h], m_ref[qi, h])
                m_new = jnp.maximum(m, jnp.max(st, axis=0, keepdims=True))
                al_out.append(jnp.exp2(m - m_new))
            p_ref[h] = jnp.exp2(st - m_new).astype(BF16)
            m_ref[qi, h] = m_new
            m_out.append(m_new)
        return (qi, m_out), al_out

    def value_stage(n, p_ref, al):
        qi, kj = _wave_pair(n, nq)
        for h in range(HEADS_PER_BLOCK):
            pv = jnp.dot(vx_ref[h, :, pl.ds(kj * TK, TK)], p_ref[h], preferred_element_type=F32)
            if (qi == kj) is True:
                acc_ref[qi, h] = pv
            else:
                acc_ref[qi, h] = al[h] * acc_ref[qi, h] + pv

    def steps(it, state, unroll=UNROLL):
        al_prv, last = state
        for u in range(unroll):
            n = it * unroll + u
            score_stage(n + 1, s_bufs[1 - u % 2])
            last, al = softmax_stage(n, s_bufs[u % 2], p_bufs[u % 2], last)
            if not (isinstance(n, int) and n == 0):
                value_stage(n - 1, p_bufs[1 - u % 2], al_prv)
            al_prv = al
        return al_prv, last

    n_static_steps = min(-(-(SKIP_DISTANCE * nq - SKIP_DISTANCE * (SKIP_DISTANCE - 1) // 2) // UNROLL),
                         n_pairs // UNROLL) * UNROLL
    score_stage(0, s_bufs[0])
    n_static = n_static_steps // UNROLL
    al_prv, last = steps(0, (None, (-1, None)), n_static_steps)
    state = (al_prv, (jnp.int32(last[0]), last[1]))

    def live(carry):
        it = carry[0]
        qi0, kj0 = _wave_pair(it * UNROLL, nq)
        nearest = qi0 - kj0
        worst = jnp.float32(-jnp.inf)
        for i in range(SKIP_DISTANCE, nq):
            last_rows = (jnp.maximum(i - nearest, 0) + 1) * TK - 16
            for h in range(HEADS_PER_BLOCK):
                bias_end = jnp.sum(kx_ref[h, pl.ds(last_rows, 16), LANES:][15:].astype(F32))
                gap = jnp.max(bnd_ref[h, :, i * TQ:(i + 1) * TQ] - m_ref[i, h])
                worst = jnp.maximum(worst, jnp.where(i >= nearest, bias_end + gap, -jnp.inf))
        return (it < n_pairs // UNROLL) & (worst > -DEAD_GAP)

    def body(carry):
        it, state = carry
        return it + 1, steps(it, state)

    it, (al_prv, _) = lax.while_loop(live, body, (jnp.int32(n_static), state))
    value_stage(it * UNROLL - 1, p_bufs[1], al_prv)
    _finalize(z_ref, o_ref, acc_ref, nq, True)


def _sb_kernel(qt_ref, k_ref, vt_ref, z_ref, tri_ref, o_ref,
               kx_ref, mk_ref, s0_ref, s1_ref, lk0_ref, lk1_ref, lb0_ref, lb1_ref,
               p0_ref, p1_ref, r_ref, acc_ref):
    s_len = k_ref.shape[1]
    nq = s_len // TQ
    n_pairs = nq * (nq + 1) // 2
    k = k_ref[0]
    for h in range(HEADS_PER_BLOCK):
        kx_ref[h] = _mask_keys_to_head(k, h)
    key = lax.broadcasted_iota(jnp.int32, (TK, TQ), 0)
    qry = lax.broadcasted_iota(jnp.int32, (TK, TQ), 1)
    mk_ref[...] = jnp.where(key < qry, 0.0, MASKED_SCORE)
    s_bufs, lk_bufs, lb_bufs, p_bufs = ((s0_ref, s1_ref), (lk0_ref, lk1_ref),
                                        (lb0_ref, lb1_ref), (p0_ref, p1_ref))

    pair = lambda n: _wave_pair(n, nq)

    def score_stage(n, s_ref):
        qi, kj = pair(n)
        w = qt_ref[0, :, pl.ds(qi * TQ, TQ)]
        for h in range(HEADS_PER_BLOCK):
            s_ref[h] = jnp.dot(kx_ref[h, pl.ds(kj * TK, TK), :], w, preferred_element_type=F32)

    def gates(zz):
        neg_abs = lax.bitcast_convert_type(
            lax.bitcast_convert_type(zz, jnp.uint32) | jnp.uint32(0x80000000), F32)
        lb = jnp.minimum(zz, 0.0) - jnp.log2(1.0 + jnp.exp2(neg_abs))
        return lb, (lb - zz).astype(BF16)

    half = TK // 2
    live_blocks = ((slice(0, half), slice(0, TQ)), (slice(half, TK), slice(half, TQ)))

    def gate_stage(n, s_ref, lk_ref, lb_ref):
        qi, kj = pair(n)
        assert isinstance(qi, int) or n_static_steps >= nq
        for h in range(HEADS_PER_BLOCK):
            if (qi == kj) is True:
                for rows, cols in live_blocks:
                    lb, lk = gates(s_ref[h, rows, cols] + mk_ref[rows, cols])
                    lb_ref[h, rows, cols] = lb
                    lk_ref[h, rows, cols] = lk
                lk_ref[h, half:, :half] = jnp.zeros((TK - half, half), BF16)
            else:
                lb_ref[h], lk_ref[h] = gates(s_ref[h])

    def weight_stage(n, lk_ref, lb_ref, p_ref):
        qi, kj = pair(n)
        tot = []
        for h in range(HEADS_PER_BLOCK):
            after = jnp.dot(tri_ref[...], lk_ref[h], preferred_element_type=F32)
            if (qi == kj) is True:
                for rows, cols in live_blocks:
                    p_ref[h, rows, cols] = jnp.exp2(lb_ref[h, rows, cols] + after[rows, cols]).astype(BF16)
                p_ref[h, half:, :half] = jnp.zeros((TK - half, half), BF16)
            else:
                p_ref[h] = jnp.exp2(lb_ref[h] + after[:TK]).astype(BF16)
            tot.append(after[TK:TK + 1])
        return tot

    def value_stage(n, p_ref, tot, last):
        qi, kj = pair(n)
        r_out = []
        for h in range(HEADS_PER_BLOCK):
            pv = jnp.dot(vt_ref[0, h * HEAD_DIM:(h + 1) * HEAD_DIM, pl.ds(kj * TK, TK)], p_ref[h],
                         preferred_element_type=F32)
            if (qi == kj) is True:
                acc_ref[qi, h] = pv
                r_new = tot[h]
            else:
                r = _pick(qi == last[0], last[1][h], r_ref[qi, h])
                acc_ref[qi, h] += jnp.exp2(r) * pv
                r_new = r + tot[h]
            r_ref[qi, h] = r_new
            r_out.append(r_new)
        return qi, r_out

    def steps(it, state, unroll=UNROLL):
        tot_prv, last = state
        for u in range(unroll):
            n = it * unroll + u
            score_stage(n + 2, s_bufs[u % 2])
            gate_stage(n + 1, s_bufs[1 - u % 2], lk_bufs[1 - u % 2], lb_bufs[1 - u % 2])
            tot = weight_stage(n, lk_bufs[u % 2], lb_bufs[u % 2], p_bufs[u % 2])
            if not (isinstance(n, int) and n == 0):
                last = value_stage(n - 1, p_bufs[1 - u % 2], tot_prv, last)
            tot_prv = tot
        return tot_prv, last

    n_static = min(-(-(2 * nq - 1) // UNROLL), n_pairs // UNROLL)
    n_static_steps = n_static * UNROLL
    score_stage(0, s_bufs[0])
    gate_stage(0, s_bufs[0], lk_bufs[0], lb_bufs[0])
    score_stage(1, s_bufs[1])

    tot_prv, last = steps(0, (None, (-1, None)), n_static_steps)
    state = (tot_prv, (jnp.int32(last[0]), last[1]))

    def live(carry):
        it = carry[0]
        if nq <= FIRST_LIVE_TILE:
            return it < 0
        return (it < n_pairs // UNROLL) & (jnp.max(jnp.exp2(r_ref[FIRST_LIVE_TILE:])) > 0.0)

    def body(carry):
        it, state = carry
        return it + 1, steps(it, state)

    it, (tot_prv, last) = lax.while_loop(live, body, (jnp.int32(n_static), state))
    value_stage(it * UNROLL - 1, p_bufs[1], tot_prv, last)
    _finalize(z_ref, o_ref, acc_ref, nq, False)


def _mixer_call(kernel, name, args, in_specs, scratch, b, s):
    seq_spec = pl.BlockSpec((1, s, LANES), lambda i, j: (i, 0, j))
    return pl.pallas_call(
        kernel,
        grid=(b, D_INNER // LANES),
        in_specs=in_specs,
        out_specs=seq_spec,
        out_shape=jax.ShapeDtypeStruct((b, s, D_INNER), BF16),
        scratch_shapes=scratch,
        compiler_params=pltpu.CompilerParams(
            dimension_semantics=("parallel", "parallel"), vmem_limit_bytes=VMEM_LIMIT),
        name=name,
    )(*args)


def _fox_mixer(qt, k, bp, vt, z):
    b, s, _ = k.shape
    nq = s // TQ
    assert (nq * (nq + 1) // 2) % UNROLL == 0
    seq_spec = pl.BlockSpec((1, s, LANES), lambda i, j: (i, 0, j))
    t_spec = pl.BlockSpec((1, LANES, s), lambda i, j: (i, j, 0))
    bp_spec = pl.BlockSpec((1, s, LANES), lambda i, j: (i, 0, 0))
    tile = (HEADS_PER_BLOCK, TK, TQ)
    scratch = [pltpu.VMEM((HEADS_PER_BLOCK, s, 2 * LANES), BF16),
               pltpu.VMEM((2 * LANES, s), BF16),
               pltpu.VMEM((HEADS_PER_BLOCK, VROWS, s), BF16),
               pltpu.VMEM((TK, TQ), F32),
               pltpu.VMEM(tile, F32), pltpu.VMEM(tile, F32),
               pltpu.VMEM(tile, BF16), pltpu.VMEM(tile, BF16),
               pltpu.VMEM((nq, HEADS_PER_BLOCK, 1, TQ), F32),
               pltpu.VMEM((nq, HEADS_PER_BLOCK, VROWS, TQ), F32),
               pltpu.VMEM((HEADS_PER_BLOCK, 1, s), F32)]
    return _mixer_call(_fox_kernel, "fox", (qt, k, bp, vt, z),
                       [t_spec, seq_spec, bp_spec, t_spec, seq_spec], scratch, b, s)


def _sb_mixer(qt, k, vt, z):
    b, s, _ = k.shape
    nq = s // TQ
    assert (nq * (nq + 1) // 2) % UNROLL == 0
    r = lax.broadcasted_iota(jnp.int32, (TRI_ROWS, TK), 0)
    c = lax.broadcasted_iota(jnp.int32, (TRI_ROWS, TK), 1)
    tri = ((c > r) | (r >= TK)).astype(BF16)
    seq_spec = pl.BlockSpec((1, s, LANES), lambda i, j: (i, 0, j))
    t_spec = pl.BlockSpec((1, LANES, s), lambda i, j: (i, j, 0))
    tri_spec = pl.BlockSpec((TRI_ROWS, TK), lambda i, j: (0, 0))
    tile = (HEADS_PER_BLOCK, TK, TQ)
    scratch = [pltpu.VMEM((HEADS_PER_BLOCK, s, LANES), BF16),
               pltpu.VMEM((TK, TQ), F32),
               pltpu.VMEM(tile, F32), pltpu.VMEM(tile, F32),
               pltpu.VMEM(tile, BF16), pltpu.VMEM(tile, BF16),
               pltpu.VMEM(tile, F32), pltpu.VMEM(tile, F32),
               pltpu.VMEM(tile, BF16), pltpu.VMEM(tile, BF16),
               pltpu.VMEM((nq, HEADS_PER_BLOCK, 1, TQ), F32),
               pltpu.VMEM((nq, HEADS_PER_BLOCK, HEAD_DIM, TQ), F32)]
    return _mixer_call(_sb_kernel, "sb", (qt, k, vt, z, tri),
                       [t_spec, seq_spec, t_spec, seq_spec, tri_spec], scratch, b, s)


def _outproj_kernel(u_ref, w_ref, x_ref, gate_ref, o_ref):
    y = jnp.dot(u_ref[0], w_ref[...], preferred_element_type=F32)
    o_ref[0] = x_ref[0] + gate_ref[0] * y


def _outproj_final_kernel(u_ref, w_ref, x_ref, gate_ref, g_ref, o_ref):
    y = jnp.dot(u_ref[0], w_ref[...], preferred_element_type=F32)
    r = x_ref[0] + gate_ref[0] * y
    o_ref[0] = r * lax.rsqrt(jnp.mean(r * r, axis=-1, keepdims=True) + NORM_EPS) * g_ref[...]


def _outproj(u, w, x, mod3, final_g=None):
    b, s, d = x.shape
    row_spec = lambda width: pl.BlockSpec((1, ROW_TILE, width), lambda i, j: (i, j, 0))
    const = lambda shape: pl.BlockSpec(shape, lambda i, j: (0,) * len(shape),
                                       pipeline_mode=pl.Buffered(1))
    in_specs = [row_spec(D_INNER), const(w.shape), row_spec(d),
                pl.BlockSpec((1, 1, d), lambda i, j: (3 * i + 2, 0, 0))]
    args = [u, w, x, mod3]
    kernel = _outproj_kernel
    if final_g is not None:
        in_specs.append(const((1, d)))
        args.append(final_g.reshape(1, d))
        kernel = _outproj_final_kernel
    return pl.pallas_call(
        kernel,
        grid=(b, s // ROW_TILE),
        in_specs=in_specs,
        out_specs=row_spec(d),
        out_shape=jax.ShapeDtypeStruct((b, s, d), F32),
        compiler_params=pltpu.CompilerParams(
            dimension_semantics=("parallel", "parallel"), vmem_limit_bytes=VMEM_LIMIT),
        name="outproj_final" if final_g is not None else "outproj",
    )(*args)


def _arrange_bias_parts(parts):
    b, s, _ = parts[0].shape
    x = jnp.stack([p[:, :, :N_HEADS] for p in parts], axis=-1)
    x = x.reshape(b, s, N_HEADS // HEADS_PER_BLOCK, HEADS_PER_BLOCK * BIAS_PARTS)
    x = jnp.pad(x, ((0, 0), (0, 0), (0, 0), (0, BIAS_STRIDE - HEADS_PER_BLOCK * BIAS_PARTS)))
    return x.reshape(b, s, LANES)


def _fox_layer(x, c, norm_g, w_ada, b_ada, w_in, b_f, w_out, final_g):
    b, s, d = x.shape
    mod3 = _adaln(c, w_ada, b_ada).reshape(b * 3, 1, d)
    wkz, wqvt = _projection_weights(w_in)
    wf = jnp.pad(w_in[:, 4 * D_INNER:], ((0, 0), (0, LANES - N_HEADS))).astype(BF16)
    bf = jnp.pad(b_f, (0, LANES - N_HEADS)).reshape(1, LANES)
    qt, k, vt, z, hi, mid, lo = _inproj(x, mod3, norm_g, wkz, wqvt, forget=(wf, bf))
    u = _fox_mixer(qt, k, _arrange_bias_parts((hi, mid, lo)), vt, z)
    return _outproj(u, w_out.astype(BF16), x, mod3, final_g)


def _sb_layer(x, c, norm_g, w_ada, b_ada, w_in, w_out, final_g):
    b, s, d = x.shape
    mod3 = _adaln(c, w_ada, b_ada).reshape(b * 3, 1, d)
    wkz, wqvt = _projection_weights(w_in)
    qt, k, vt, z = _inproj(x, mod3, norm_g, wkz, wqvt)
    u = _sb_mixer(qt, k, vt, z)
    return _outproj(u, w_out.astype(BF16), x, mod3, final_g)


def kernel(x, c, fox_norm_g, fox_w_ada, fox_b_ada, fox_w_in, fox_b_f, fox_w_out,
           sb_norm_g, sb_w_ada, sb_b_ada, sb_w_in, sb_w_out, final_norm_g):
    depth = fox_norm_g.shape[0] + sb_norm_g.shape[0]
    for i in range(depth):
        j = i // 2
        final_g = final_norm_g if i == depth - 1 else None
        if i % 2 == 0:
            x = _fox_layer(x, c, fox_norm_g[j], fox_w_ada[j], fox_b_ada[j],
                           fox_w_in[j], fox_b_f[j], fox_w_out[j], final_g)
        else:
            x = _sb_layer(x, c, sb_norm_g[j], sb_w_ada[j], sb_b_ada[j],
                          sb_w_in[j], sb_w_out[j], final_g)
    return x
```

```python
import math

import jax
import jax.numpy as jnp
from jax import lax
from jax.experimental import pallas as pl
from jax.experimental.pallas import tpu as pltpu

D_MODEL = 1024
D_INNER = 2048
HEAD_DIM = 64
N_HEADS = D_INNER // HEAD_DIM
NORM_EPS = 1e-6
LOG2E = math.log2(math.e)
QK_SCALE = HEAD_DIM ** -0.5 * LOG2E

LANES = 128
HEADS_PER_BLOCK = LANES // HEAD_DIM
VMEM_LIMIT = 56 * 1024 * 1024

ROW_TILE = 512
COL_CHUNK = 512
ADA_COLS = 512
TQ = 256
TK = 256
UNROLL = 4
VROWS = HEAD_DIM + 16
BIAS_PARTS = 3
BIAS_STRIDE = 8
MASKED_SCORE = -1e30
TRI_ROWS = TK + 16
FIRST_LIVE_TILE = 3
SKIP_DISTANCE = 3
DEAD_GAP = 140.0

F32 = jnp.float32
BF16 = jnp.bfloat16


def _split3(x):
    hi = x.astype(BF16)
    r1 = x - hi.astype(F32)
    mid = r1.astype(BF16)
    lo = (r1 - mid.astype(F32)).astype(BF16)
    return hi, mid, lo


def _log_sigmoid(x):
    return jnp.minimum(x, 0.0) - jnp.log1p(jnp.exp(-jnp.abs(x)))


def _adaln_kernel(c_ref, w_ref, b_ref, o_ref):
    c = c_ref[...]
    s = c * jax.nn.sigmoid(c)
    o_ref[...] = jnp.dot(s, w_ref[...], preferred_element_type=F32,
                         precision=lax.Precision.HIGHEST) + b_ref[...]


def _adaln(c, w_ada, b_ada):
    b, d = c.shape
    n = w_ada.shape[1]
    return pl.pallas_call(
        _adaln_kernel,
        grid=(n // ADA_COLS,),
        in_specs=[pl.BlockSpec((b, d), lambda j: (0, 0)),
                  pl.BlockSpec((d, ADA_COLS), lambda j: (0, j)),
                  pl.BlockSpec((1, ADA_COLS), lambda j: (0, j))],
        out_specs=pl.BlockSpec((b, ADA_COLS), lambda j: (0, j)),
        out_shape=jax.ShapeDtypeStruct((b, n), F32),
        name="adaln",
    )(c, w_ada, b_ada.reshape(1, n))


def _modulated_norm(x_ref, shift_ref, scale_ref, g_ref):
    x = x_ref[0]
    y = x * lax.rsqrt(jnp.mean(x * x, axis=-1, keepdims=True) + NORM_EPS) * g_ref[...]
    return (y * (1.0 + scale_ref[0]) + shift_ref[0]).astype(BF16)


def _project(h, wkz_ref, wqvt_ref, qt_ref, k_ref, vt_ref, z_ref):
    chunks = D_INNER // COL_CHUNK
    for o, out_ref in enumerate((k_ref, z_ref)):
        for cc in range(chunks):
            col = o * D_INNER + cc * COL_CHUNK
            out_ref[0, :, cc * COL_CHUNK:(cc + 1) * COL_CHUNK] = jnp.dot(
                h, wkz_ref[:, col:col + COL_CHUNK], preferred_element_type=F32).astype(BF16)
    for o, (out_ref, scale) in enumerate(((qt_ref, QK_SCALE), (vt_ref, None))):
        for cc in range(chunks):
            row = o * D_INNER + cc * COL_CHUNK
            t = lax.dot_general(wqvt_ref[row:row + COL_CHUNK, :], h, (((1,), (1,)), ((), ())),
                                preferred_element_type=F32)
            if scale is not None:
                t = t * scale
            out_ref[0, cc * COL_CHUNK:(cc + 1) * COL_CHUNK, :] = t.astype(BF16)


def _inproj_sb_kernel(x_ref, shift_ref, scale_ref, g_ref, wkz_ref, wqvt_ref,
                      qt_ref, k_ref, vt_ref, z_ref):
    h = _modulated_norm(x_ref, shift_ref, scale_ref, g_ref)
    _project(h, wkz_ref, wqvt_ref, qt_ref, k_ref, vt_ref, z_ref)


def _inproj_fox_kernel(x_ref, shift_ref, scale_ref, g_ref, wkz_ref, wqvt_ref, wf_ref, bf_ref, perm_ref,
                       qt_ref, k_ref, vt_ref, z_ref, bp_ref, carry_ref):
    h = _modulated_norm(x_ref, shift_ref, scale_ref, g_ref)
    _project(h, wkz_ref, wqvt_ref, qt_ref, k_ref, vt_ref, z_ref)

    @pl.when(pl.program_id(1) == 0)
    def _():
        carry_ref[...] = jnp.zeros_like(carry_ref)

    log_f = _log_sigmoid(jnp.dot(h, wf_ref[...], preferred_element_type=F32) + bf_ref[...])
    rows = log_f.shape[0]
    r = lax.broadcasted_iota(jnp.int32, (rows, rows), 0)
    c = lax.broadcasted_iota(jnp.int32, (rows, rows), 1)
    tri = (c <= r).astype(BF16)
    hi, mid, lo = _split3(log_f)
    cum = (jnp.dot(tri, hi, preferred_element_type=F32)
           + jnp.dot(tri, mid, preferred_element_type=F32)
           + jnp.dot(tri, lo, preferred_element_type=F32)) + carry_ref[...]
    carry_ref[...] = cum[rows - 1:rows, :]
    parts = _split3(cum * (-LOG2E))
    bp_ref[0] = sum(jnp.dot(part, perm_ref[i], preferred_element_type=F32)
                    for i, part in enumerate(parts)).astype(BF16)


def _inproj(x, mod3, g, wkz, wqvt, forget=None):
    b, s, d = x.shape
    row_spec = lambda width: pl.BlockSpec((1, ROW_TILE, width), lambda i, j: (i, j, 0))
    col_spec = pl.BlockSpec((1, D_INNER, ROW_TILE), lambda i, j: (i, 0, j))
    mod_spec = lambda k: pl.BlockSpec((1, 1, d), lambda i, j: (3 * i + k, 0, 0))
    const = lambda shape: pl.BlockSpec(shape, lambda i, j: (0,) * len(shape),
                                       pipeline_mode=pl.Buffered(1))
    in_specs = [row_spec(d), mod_spec(0), mod_spec(1), const((1, d)),
                const(wkz.shape), const(wqvt.shape)]
    args = [x, mod3, mod3, g.reshape(1, d), wkz, wqvt]
    row_major = jax.ShapeDtypeStruct((b, s, D_INNER), BF16)
    col_major = jax.ShapeDtypeStruct((b, D_INNER, s), BF16)
    out_specs = [col_spec, row_spec(D_INNER), col_spec, row_spec(D_INNER)]
    out_shape = [col_major, row_major, col_major, row_major]
    scratch = []
    kernel = _inproj_sb_kernel
    if forget is not None:
        wf, bf, perm = forget
        in_specs += [const(wf.shape), const(bf.shape), const(perm.shape)]
        args += [wf, bf, perm]
        out_specs.append(row_spec(LANES))
        out_shape.append(jax.ShapeDtypeStruct((b, s, LANES), BF16))
        scratch = [pltpu.VMEM((1, LANES), F32)]
        kernel = _inproj_fox_kernel
    return pl.pallas_call(
        kernel,
        grid=(b, s // ROW_TILE),
        in_specs=in_specs,
        out_specs=out_specs,
        out_shape=out_shape,
        scratch_shapes=scratch,
        compiler_params=pltpu.CompilerParams(
            dimension_semantics=("parallel", "arbitrary"), vmem_limit_bytes=VMEM_LIMIT),
        name="inproj_fox" if forget is not None else "inproj_sb",
    )(*args)


def _projection_weights(w_in):
    q, k, v, z = (w_in[:, i * D_INNER:(i + 1) * D_INNER] for i in range(4))
    return (jnp.concatenate([k, z], axis=1).astype(BF16),
            jnp.concatenate([q, v], axis=1).T.astype(BF16))


def _mask_keys_to_head(k, h):
    lane = lax.broadcasted_iota(jnp.int32, k.shape, 1)
    return jnp.where(lane // HEAD_DIM == h, k, jnp.zeros_like(k))


def _finalize(z_ref, o_ref, acc_ref, nq, normalize):
    for i in range(nq):
        outs = []
        for h in range(HEADS_PER_BLOCK):
            acc = acc_ref[i, h]
            o = acc[:HEAD_DIM]
            if normalize:
                o = o / acc[HEAD_DIM:HEAD_DIM + 1]
            outs.append(o)
        o = jnp.concatenate(outs, axis=0).T
        z = z_ref[0, i * TQ:(i + 1) * TQ, :].astype(F32)
        o_ref[0, i * TQ:(i + 1) * TQ, :] = (o * (z * jax.nn.sigmoid(z))).astype(o_ref.dtype)


def _wave_pair(n, nq):
    n_pairs = nq * (nq + 1) // 2
    starts = [e * nq - e * (e - 1) // 2 for e in range(nq)]
    if isinstance(n, int):
        n = min(max(n, 0), n_pairs - 1)
        d = max(e for e in range(nq) if starts[e] <= n)
        return n - starts[d] + d, n - starts[d]
    n = jnp.clip(n, 0, n_pairs - 1)
    d = jnp.int32(0)
    for e in range(1, nq):
        d = d + (n >= starts[e]).astype(jnp.int32)
    qi = n - (d * nq - d * (d - 1) // 2) + d
    return qi, qi - d


def _pick(pred, a, b):
    if isinstance(pred, bool):
        return a if pred else b
    return jnp.where(pred, a, b)


def _fox_kernel(qt_ref, k_ref, bp_ref, vt_ref, z_ref, o_ref,
                kx_ref, w_ref, vx_ref, mk_ref, s0_ref, s1_ref, p0_ref, p1_ref, m_ref, acc_ref, bnd_ref):
    s_len = k_ref.shape[1]
    nq = s_len // TQ
    n_pairs = nq * (nq + 1) // 2
    hp = pl.program_id(1)
    k = k_ref[0]
    bp = bp_ref[0]
    lane = lax.broadcasted_iota(jnp.int32, bp.shape, 1)
    for h in range(HEADS_PER_BLOCK):
        base = BIAS_STRIDE * hp + BIAS_PARTS * h
        kx_ref[h, :, :LANES] = _mask_keys_to_head(k, h)
        kx_ref[h, :, LANES:] = jnp.where((lane >= base) & (lane < base + BIAS_PARTS),
                                         bp, jnp.zeros_like(bp))
        vx_ref[h, :HEAD_DIM, :] = vt_ref[0, h * HEAD_DIM:(h + 1) * HEAD_DIM, :]
        vx_ref[h, HEAD_DIM:, :] = jnp.ones((VROWS - HEAD_DIM, s_len), BF16)
    w_ref[:LANES, :] = qt_ref[0]
    w_ref[LANES:, :] = jnp.ones((LANES, s_len), BF16)
    k_abs = jnp.max(jnp.abs(k.astype(F32)), axis=0, keepdims=True)
    lane1 = lax.broadcasted_iota(jnp.int32, (8, LANES), 1)
    k_abs = jnp.concatenate([jnp.where(lane1 // HEAD_DIM == h, k_abs, 0.0)
                             for h in range(HEADS_PER_BLOCK)], axis=0).astype(BF16)
    bound = jnp.dot(k_abs, jnp.abs(qt_ref[0]), preferred_element_type=F32)
    for h in range(HEADS_PER_BLOCK):
        bnd_ref[h] = bound[8 * h:8 * h + 1]
    key = lax.broadcasted_iota(jnp.int32, (TK, TQ), 0)
    qry = lax.broadcasted_iota(jnp.int32, (TK, TQ), 1)
    mk_ref[...] = jnp.where(key <= qry, 0.0, MASKED_SCORE)
    s_bufs, p_bufs = (s0_ref, s1_ref), (p0_ref, p1_ref)

    def score_stage(n, s_ref):
        qi, kj = _wave_pair(n, nq)
        w = w_ref[:, pl.ds(qi * TQ, TQ)]
        for h in range(HEADS_PER_BLOCK):
            s_ref[h] = jnp.dot(kx_ref[h, pl.ds(kj * TK, TK), :], w, preferred_element_type=F32)

    def softmax_stage(n, s_ref, p_ref, last):
        qi, kj = _wave_pair(n, nq)
        first = qi == kj
        assert isinstance(first, bool) or n_static_steps >= nq
        m_out, al_out = [], []
        for h in range(HEADS_PER_BLOCK):
            if first is True:
                st = s_ref[h] + mk_ref[...]
                m_new = jnp.max(st, axis=0, keepdims=True)
                al_out.append(None)
            else:
                st = s_ref[h]
                m = _pick(qi == last[0], last[1][h], m_ref[qi, h])
                m_new = jnp.maximum(m, jnp.max(st, axis=0, keepdims=True))
                al_out.append(jnp.exp2(m - m_new))
            p_ref[h] = jnp.exp2(st - m_new).astype(BF16)
            m_ref[qi, h] = m_new
            m_out.append(m_new)
        return (qi, m_out), al_out

    def value_stage(n, p_ref, al):
        qi, kj = _wave_pair(n, nq)
        for h in range(HEADS_PER_BLOCK):
            pv = jnp.dot(vx_ref[h, :, pl.ds(kj * TK, TK)], p_ref[h], preferred_element_type=F32)
            if (qi == kj) is True:
                acc_ref[qi, h] = pv
            else:
                acc_ref[qi, h] = al[h] * acc_ref[qi, h] + pv

    def steps(it, state, unroll=UNROLL):
        al_prv, last = state
        for u in range(unroll):
            n = it * unroll + u
            score_stage(n + 1, s_bufs[1 - u % 2])
            last, al = softmax_stage(n, s_bufs[u % 2], p_bufs[u % 2], last)
            if not (isinstance(n, int) and n == 0):
                value_stage(n - 1, p_bufs[1 - u % 2], al_prv)
            al_prv = al
        return al_prv, last

    n_static_steps = min(-(-(SKIP_DISTANCE * nq - SKIP_DISTANCE * (SKIP_DISTANCE - 1) // 2) // UNROLL),
                         n_pairs // UNROLL) * UNROLL
    score_stage(0, s_bufs[0])
    n_static = n_static_steps // UNROLL
    al_prv, last = steps(0, (None, (-1, None)), n_static_steps)
    state = (al_prv, (jnp.int32(last[0]), last[1]))

    def live(carry):
        it = carry[0]
        qi0, kj0 = _wave_pair(it * UNROLL, nq)
        nearest = qi0 - kj0
        worst = jnp.float32(-jnp.inf)
        for i in range(SKIP_DISTANCE, nq):
            last_rows = (jnp.maximum(i - nearest, 0) + 1) * TK - 16
            for h in range(HEADS_PER_BLOCK):
                bias_end = jnp.sum(kx_ref[h, pl.ds(last_rows, 16), LANES:][15:].astype(F32))
                gap = jnp.max(bnd_ref[h, :, i * TQ:(i + 1) * TQ] - m_ref[i, h])
                worst = jnp.maximum(worst, jnp.where(i >= nearest, bias_end + gap, -jnp.inf))
        return (it < n_pairs // UNROLL) & (worst > -DEAD_GAP)

    def body(carry):
        it, state = carry
        return it + 1, steps(it, state)

    it, (al_prv, _) = lax.while_loop(live, body, (jnp.int32(n_static), state))
    value_stage(it * UNROLL - 1, p_bufs[1], al_prv)
    _finalize(z_ref, o_ref, acc_ref, nq, True)


def _sb_kernel(qt_ref, k_ref, vt_ref, z_ref, tri_ref, o_ref,
               kx_ref, mk_ref, s0_ref, s1_ref, lk0_ref, lk1_ref, lb0_ref, lb1_ref,
               p0_ref, p1_ref, r_ref, acc_ref):
    s_len = k_ref.shape[1]
    nq = s_len // TQ
    n_pairs = nq * (nq + 1) // 2
    k = k_ref[0]
    for h in range(HEADS_PER_BLOCK):
        kx_ref[h] = _mask_keys_to_head(k, h)
    key = lax.broadcasted_iota(jnp.int32, (TK, TQ), 0)
    qry = lax.broadcasted_iota(jnp.int32, (TK, TQ), 1)
    mk_ref[...] = jnp.where(key < qry, 0.0, MASKED_SCORE)
    s_bufs, lk_bufs, lb_bufs, p_bufs = ((s0_ref, s1_ref), (lk0_ref, lk1_ref),
                                        (lb0_ref, lb1_ref), (p0_ref, p1_ref))

    pair = lambda n: _wave_pair(n, nq)

    def score_stage(n, s_ref):
        qi, kj = pair(n)
        w = qt_ref[0, :, pl.ds(qi * TQ, TQ)]
        for h in range(HEADS_PER_BLOCK):
            s_ref[h] = jnp.dot(kx_ref[h, pl.ds(kj * TK, TK), :], w, preferred_element_type=F32)

    def gates(zz):
        neg_abs = lax.bitcast_convert_type(
            lax.bitcast_convert_type(zz, jnp.uint32) | jnp.uint32(0x80000000), F32)
        lb = jnp.minimum(zz, 0.0) - jnp.log2(1.0 + jnp.exp2(neg_abs))
        return lb, (lb - zz).astype(BF16)

    half = TK // 2
    live_blocks = ((slice(0, half), slice(0, TQ)), (slice(half, TK), slice(half, TQ)))

    def gate_stage(n, s_ref, lk_ref, lb_ref):
        qi, kj = pair(n)
        assert isinstance(qi, int) or n_static_steps >= nq
        for h in range(HEADS_PER_BLOCK):
            if (qi == kj) is True:
                for rows, cols in live_blocks:
                    lb, lk = gates(s_ref[h, rows, cols] + mk_ref[rows, cols])
                    lb_ref[h, rows, cols] = lb
                    lk_ref[h, rows, cols] = lk
                lk_ref[h, half:, :half] = jnp.zeros((TK - half, half), BF16)
            else:
                lb_ref[h], lk_ref[h] = gates(s_ref[h])

    def weight_stage(n, lk_ref, lb_ref, p_ref):
        qi, kj = pair(n)
        tot = []
        for h in range(HEADS_PER_BLOCK):
            after = jnp.dot(tri_ref[...], lk_ref[h], preferred_element_type=F32)
            if (qi == kj) is True:
                for rows, cols in live_blocks:
                    p_ref[h, rows, cols] = jnp.exp2(lb_ref[h, rows, cols] + after[rows, cols]).astype(BF16)
                p_ref[h, half:, :half] = jnp.zeros((TK - half, half), BF16)
            else:
                p_ref[h] = jnp.exp2(lb_ref[h] + after[:TK]).astype(BF16)
            tot.append(after[TK:TK + 1])
        return tot

    def value_stage(n, p_ref, tot, last):
        qi, kj = pair(n)
        r_out = []
        for h in range(HEADS_PER_BLOCK):
            pv = jnp.dot(vt_ref[0, h * HEAD_DIM:(h + 1) * HEAD_DIM, pl.ds(kj * TK, TK)], p_ref[h],
                         preferred_element_type=F32)
            if (qi == kj) is True:
                acc_ref[qi, h] = pv
                r_new = tot[h]
            else:
                r = _pick(qi == last[0], last[1][h], r_ref[qi, h])
                acc_ref[qi, h] += jnp.exp2(r) * pv
                r_new = r + tot[h]
            r_ref[qi, h] = r_new
            r_out.append(r_new)
        return qi, r_out

    def steps(it, state, unroll=UNROLL):
        tot_prv, last = state
        for u in range(unroll):
            n = it * unroll + u
            score_stage(n + 2, s_bufs[u % 2])
            gate_stage(n + 1, s_bufs[1 - u % 2], lk_bufs[1 - u % 2], lb_bufs[1 - u % 2])
            tot = weight_stage(n, lk_bufs[u % 2], lb_bufs[u % 2], p_bufs[u % 2])
            if not (isinstance(n, int) and n == 0):
                last = value_stage(n - 1, p_bufs[1 - u % 2], tot_prv, last)
            tot_prv = tot
        return tot_prv, last

    n_static = min(-(-(2 * nq - 1) // UNROLL), n_pairs // UNROLL)
    n_static_steps = n_static * UNROLL
    score_stage(0, s_bufs[0])
    gate_stage(0, s_bufs[0], lk_bufs[0], lb_bufs[0])
    score_stage(1, s_bufs[1])

    tot_prv, last = steps(0, (None, (-1, None)), n_static_steps)
    state = (tot_prv, (jnp.int32(last[0]), last[1]))

    def live(carry):
        it = carry[0]
        if nq <= FIRST_LIVE_TILE:
            return it < 0
        return (it < n_pairs // UNROLL) & (jnp.max(jnp.exp2(r_ref[FIRST_LIVE_TILE:])) > 0.0)

    def body(carry):
        it, state = carry
        return it + 1, steps(it, state)

    it, (tot_prv, last) = lax.while_loop(live, body, (jnp.int32(n_static), state))
    value_stage(it * UNROLL - 1, p_bufs[1], tot_prv, last)
    _finalize(z_ref, o_ref, acc_ref, nq, False)


def _mixer_call(kernel, name, args, in_specs, scratch, b, s):
    seq_spec = pl.BlockSpec((1, s, LANES), lambda i, j: (i, 0, j))
    return pl.pallas_call(
        kernel,
        grid=(b, D_INNER // LANES),
        in_specs=in_specs,
        out_specs=seq_spec,
        out_shape=jax.ShapeDtypeStruct((b, s, D_INNER), BF16),
        scratch_shapes=scratch,
        compiler_params=pltpu.CompilerParams(
            dimension_semantics=("parallel", "parallel"), vmem_limit_bytes=VMEM_LIMIT),
        name=name,
    )(*args)


def _fox_mixer(qt, k, bp, vt, z):
    b, s, _ = k.shape
    nq = s // TQ
    assert (nq * (nq + 1) // 2) % UNROLL == 0
    seq_spec = pl.BlockSpec((1, s, LANES), lambda i, j: (i, 0, j))
    t_spec = pl.BlockSpec((1, LANES, s), lambda i, j: (i, j, 0))
    bp_spec = pl.BlockSpec((1, s, LANES), lambda i, j: (i, 0, 0))
    tile = (HEADS_PER_BLOCK, TK, TQ)
    scratch = [pltpu.VMEM((HEADS_PER_BLOCK, s, 2 * LANES), BF16),
               pltpu.VMEM((2 * LANES, s), BF16),
               pltpu.VMEM((HEADS_PER_BLOCK, VROWS, s), BF16),
               pltpu.VMEM((TK, TQ), F32),
               pltpu.VMEM(tile, F32), pltpu.VMEM(tile, F32),
               pltpu.VMEM(tile, BF16), pltpu.VMEM(tile, BF16),
               pltpu.VMEM((nq, HEADS_PER_BLOCK, 1, TQ), F32),
               pltpu.VMEM((nq, HEADS_PER_BLOCK, VROWS, TQ), F32),
               pltpu.VMEM((HEADS_PER_BLOCK, 1, s), F32)]
    return _mixer_call(_fox_kernel, "fox", (qt, k, bp, vt, z),
                       [t_spec, seq_spec, bp_spec, t_spec, seq_spec], scratch, b, s)


def _sb_mixer(qt, k, vt, z):
    b, s, _ = k.shape
    nq = s // TQ
    assert (nq * (nq + 1) // 2) % UNROLL == 0
    r = lax.broadcasted_iota(jnp.int32, (TRI_ROWS, TK), 0)
    c = lax.broadcasted_iota(jnp.int32, (TRI_ROWS, TK), 1)
    tri = ((c > r) | (r >= TK)).astype(BF16)
    seq_spec = pl.BlockSpec((1, s, LANES), lambda i, j: (i, 0, j))
    t_spec = pl.BlockSpec((1, LANES, s), lambda i, j: (i, j, 0))
    tri_spec = pl.BlockSpec((TRI_ROWS, TK), lambda i, j: (0, 0))
    tile = (HEADS_PER_BLOCK, TK, TQ)
    scratch = [pltpu.VMEM((HEADS_PER_BLOCK, s, LANES), BF16),
               pltpu.VMEM((TK, TQ), F32),
               pltpu.VMEM(tile, F32), pltpu.VMEM(tile, F32),
               pltpu.VMEM(tile, BF16), pltpu.VMEM(tile, BF16),
               pltpu.VMEM(tile, F32), pltpu.VMEM(tile, F32),
               pltpu.VMEM(tile, BF16), pltpu.VMEM(tile, BF16),
               pltpu.VMEM((nq, HEADS_PER_BLOCK, 1, TQ), F32),
               pltpu.VMEM((nq, HEADS_PER_BLOCK, HEAD_DIM, TQ), F32)]
    return _mixer_call(_sb_kernel, "sb", (qt, k, vt, z, tri),
                       [t_spec, seq_spec, t_spec, seq_spec, tri_spec], scratch, b, s)


def _outproj_kernel(u_ref, w_ref, x_ref, gate_ref, o_ref):
    y = jnp.dot(u_ref[0], w_ref[...], preferred_element_type=F32)
    o_ref[0] = x_ref[0] + gate_ref[0] * y


def _outproj_final_kernel(u_ref, w_ref, x_ref, gate_ref, g_ref, o_ref):
    y = jnp.dot(u_ref[0], w_ref[...], preferred_element_type=F32)
    r = x_ref[0] + gate_ref[0] * y
    o_ref[0] = r * lax.rsqrt(jnp.mean(r * r, axis=-1, keepdims=True) + NORM_EPS) * g_ref[...]


def _outproj(u, w, x, mod3, final_g=None):
    b, s, d = x.shape
    row_spec = lambda width: pl.BlockSpec((1, ROW_TILE, width), lambda i, j: (i, j, 0))
    const = lambda shape: pl.BlockSpec(shape, lambda i, j: (0,) * len(shape),
                                       pipeline_mode=pl.Buffered(1))
    in_specs = [row_spec(D_INNER), const(w.shape), row_spec(d),
                pl.BlockSpec((1, 1, d), lambda i, j: (3 * i + 2, 0, 0))]
    args = [u, w, x, mod3]
    kernel = _outproj_kernel
    if final_g is not None:
        in_specs.append(const((1, d)))
        args.append(final_g.reshape(1, d))
        kernel = _outproj_final_kernel
    return pl.pallas_call(
        kernel,
        grid=(b, s // ROW_TILE),
        in_specs=in_specs,
        out_specs=row_spec(d),
        out_shape=jax.ShapeDtypeStruct((b, s, d), F32),
        compiler_params=pltpu.CompilerParams(
            dimension_semantics=("parallel", "parallel"), vmem_limit_bytes=VMEM_LIMIT),
        name="outproj_final" if final_g is not None else "outproj",
    )(*args)


def _bias_lane_permutations():
    src = lax.broadcasted_iota(jnp.int32, (BIAS_PARTS, LANES, LANES), 1)
    dst = lax.broadcasted_iota(jnp.int32, (BIAS_PARTS, LANES, LANES), 2)
    part = lax.broadcasted_iota(jnp.int32, (BIAS_PARTS, LANES, LANES), 0)
    want = BIAS_STRIDE * (src // HEADS_PER_BLOCK) + BIAS_PARTS * (src % HEADS_PER_BLOCK) + part
    return ((dst == want) & (src < N_HEADS)).astype(BF16)


def _fox_layer(x, c, norm_g, w_ada, b_ada, w_in, b_f, w_out, final_g):
    b, s, d = x.shape
    mod3 = _adaln(c, w_ada, b_ada).reshape(b * 3, 1, d)
    wkz, wqvt = _projection_weights(w_in)
    wf = jnp.pad(w_in[:, 4 * D_INNER:], ((0, 0), (0, LANES - N_HEADS))).astype(BF16)
    bf = jnp.pad(b_f, (0, LANES - N_HEADS)).reshape(1, LANES)
    qt, k, vt, z, bp = _inproj(x, mod3, norm_g, wkz, wqvt, forget=(wf, bf, _bias_lane_permutations()))
    u = _fox_mixer(qt, k, bp, vt, z)
    return _outproj(u, w_out.astype(BF16), x, mod3, final_g)


def _sb_layer(x, c, norm_g, w_ada, b_ada, w_in, w_out, final_g):
    b, s, d = x.shape
    mod3 = _adaln(c, w_ada, b_ada).reshape(b * 3, 1, d)
    wkz, wqvt = _projection_weights(w_in)
    qt, k, vt, z = _inproj(x, mod3, norm_g, wkz, wqvt)
    u = _sb_mixer(qt, k, vt, z)
    return _outproj(u, w_out.astype(BF16), x, mod3, final_g)


def kernel(x, c, fox_norm_g, fox_w_ada, fox_b_ada, fox_w_in, fox_b_f, fox_w_out,
           sb_norm_g, sb_w_ada, sb_b_ada, sb_w_in, sb_w_out, final_norm_g):
    depth = fox_norm_g.shape[0] + sb_norm_g.shape[0]
    for i in range(depth):
        j = i // 2
        final_g = final_norm_g if i == depth - 1 else None
        if i % 2 == 0:
            x = _fox_layer(x, c, fox_norm_g[j], fox_w_ada[j], fox_b_ada[j],
                           fox_w_in[j], fox_b_f[j], fox_w_out[j], final_g)
        else:
            x = _sb_layer(x, c, sb_norm_g[j], sb_w_ada[j], sb_b_ada[j],
                          sb_w_in[j], sb_w_out[j], final_g)
    return x
```

```python
import math

import jax
import jax.numpy as jnp
from jax import lax
from jax.experimental import pallas as pl
from jax.experimental.pallas import tpu as pltpu

D_MODEL = 1024
D_INNER = 2048
HEAD_DIM = 64
N_HEADS = D_INNER // HEAD_DIM
NORM_EPS = 1e-6
LOG2E = math.log2(math.e)
QK_SCALE = HEAD_DIM ** -0.5 * LOG2E

LANES = 128
HEADS_PER_BLOCK = LANES // HEAD_DIM
VMEM_LIMIT = 56 * 1024 * 1024

ROW_TILE = 512
COL_CHUNK = 512
ADA_COLS = 512
TQ = 256
TK = 256
UNROLL = 4
VROWS = HEAD_DIM + 16
BIAS_PARTS = 3
BIAS_STRIDE = 8
MASKED_SCORE = -1e30
TRI_ROWS = TK + 16
FIRST_LIVE_TILE = 2
SKIP_DISTANCE = 3
DEAD_GAP = 140.0

F32 = jnp.float32
BF16 = jnp.bfloat16


def _split3(x):
    hi = x.astype(BF16)
    r1 = x - hi.astype(F32)
    mid = r1.astype(BF16)
    lo = (r1 - mid.astype(F32)).astype(BF16)
    return hi, mid, lo


def _log_sigmoid(x):
    return jnp.minimum(x, 0.0) - jnp.log1p(jnp.exp(-jnp.abs(x)))


def _adaln_kernel(c_ref, w_ref, b_ref, o_ref):
    c = c_ref[...]
    s = c * jax.nn.sigmoid(c)
    o_ref[...] = jnp.dot(s, w_ref[...], preferred_element_type=F32,
                         precision=lax.Precision.HIGHEST) + b_ref[...]


def _adaln(c, w_ada, b_ada):
    b, d = c.shape
    n = w_ada.shape[1]
    return pl.pallas_call(
        _adaln_kernel,
        grid=(n // ADA_COLS,),
        in_specs=[pl.BlockSpec((b, d), lambda j: (0, 0)),
                  pl.BlockSpec((d, ADA_COLS), lambda j: (0, j)),
                  pl.BlockSpec((1, ADA_COLS), lambda j: (0, j))],
        out_specs=pl.BlockSpec((b, ADA_COLS), lambda j: (0, j)),
        out_shape=jax.ShapeDtypeStruct((b, n), F32),
        name="adaln",
    )(c, w_ada, b_ada.reshape(1, n))


def _modulated_norm(x_ref, shift_ref, scale_ref, g_ref):
    x = x_ref[0]
    y = x * lax.rsqrt(jnp.mean(x * x, axis=-1, keepdims=True) + NORM_EPS) * g_ref[...]
    return (y * (1.0 + scale_ref[0]) + shift_ref[0]).astype(BF16)


def _project(h, wkz_ref, wqvt_ref, qt_ref, k_ref, vt_ref, z_ref):
    chunks = D_INNER // COL_CHUNK
    for o, out_ref in enumerate((k_ref, z_ref)):
        for cc in range(chunks):
            col = o * D_INNER + cc * COL_CHUNK
            out_ref[0, :, cc * COL_CHUNK:(cc + 1) * COL_CHUNK] = jnp.dot(
                h, wkz_ref[:, col:col + COL_CHUNK], preferred_element_type=F32).astype(BF16)
    for o, (out_ref, scale) in enumerate(((qt_ref, QK_SCALE), (vt_ref, None))):
        for cc in range(chunks):
            row = o * D_INNER + cc * COL_CHUNK
            t = lax.dot_general(wqvt_ref[row:row + COL_CHUNK, :], h, (((1,), (1,)), ((), ())),
                                preferred_element_type=F32)
            if scale is not None:
                t = t * scale
            out_ref[0, cc * COL_CHUNK:(cc + 1) * COL_CHUNK, :] = t.astype(BF16)


def _inproj_sb_kernel(x_ref, shift_ref, scale_ref, g_ref, wkz_ref, wqvt_ref,
                      qt_ref, k_ref, vt_ref, z_ref):
    h = _modulated_norm(x_ref, shift_ref, scale_ref, g_ref)
    _project(h, wkz_ref, wqvt_ref, qt_ref, k_ref, vt_ref, z_ref)


def _inproj_fox_kernel(x_ref, shift_ref, scale_ref, g_ref, wkz_ref, wqvt_ref, wf_ref, bf_ref, perm_ref,
                       qt_ref, k_ref, vt_ref, z_ref, bp_ref, carry_ref):
    h = _modulated_norm(x_ref, shift_ref, scale_ref, g_ref)
    _project(h, wkz_ref, wqvt_ref, qt_ref, k_ref, vt_ref, z_ref)

    @pl.when(pl.program_id(1) == 0)
    def _():
        carry_ref[...] = jnp.zeros_like(carry_ref)

    log_f = _log_sigmoid(jnp.dot(h, wf_ref[...], preferred_element_type=F32) + bf_ref[...])
    rows = log_f.shape[0]
    r = lax.broadcasted_iota(jnp.int32, (rows, rows), 0)
    c = lax.broadcasted_iota(jnp.int32, (rows, rows), 1)
    tri = (c <= r).astype(BF16)
    hi, mid, lo = _split3(log_f)
    cum = (jnp.dot(tri, hi, preferred_element_type=F32)
           + jnp.dot(tri, mid, preferred_element_type=F32)
           + jnp.dot(tri, lo, preferred_element_type=F32)) + carry_ref[...]
    carry_ref[...] = cum[rows - 1:rows, :]
    parts = _split3(cum * (-LOG2E))
    bp_ref[0] = sum(jnp.dot(part, perm_ref[i], preferred_element_type=F32)
                    for i, part in enumerate(parts)).astype(BF16)


def _inproj(x, mod3, g, wkz, wqvt, forget=None):
    b, s, d = x.shape
    row_spec = lambda width: pl.BlockSpec((1, ROW_TILE, width), lambda i, j: (i, j, 0))
    col_spec = pl.BlockSpec((1, D_INNER, ROW_TILE), lambda i, j: (i, 0, j))
    mod_spec = lambda k: pl.BlockSpec((1, 1, d), lambda i, j: (3 * i + k, 0, 0))
    const = lambda shape: pl.BlockSpec(shape, lambda i, j: (0,) * len(shape),
                                       pipeline_mode=pl.Buffered(1))
    in_specs = [row_spec(d), mod_spec(0), mod_spec(1), const((1, d)),
                const(wkz.shape), const(wqvt.shape)]
    args = [x, mod3, mod3, g.reshape(1, d), wkz, wqvt]
    row_major = jax.ShapeDtypeStruct((b, s, D_INNER), BF16)
    col_major = jax.ShapeDtypeStruct((b, D_INNER, s), BF16)
    out_specs = [col_spec, row_spec(D_INNER), col_spec, row_spec(D_INNER)]
    out_shape = [col_major, row_major, col_major, row_major]
    scratch = []
    kernel = _inproj_sb_kernel
    if forget is not None:
        wf, bf, perm = forget
        in_specs += [const(wf.shape), const(bf.shape), const(perm.shape)]
        args += [wf, bf, perm]
        out_specs.append(row_spec(LANES))
        out_shape.append(jax.ShapeDtypeStruct((b, s, LANES), BF16))
        scratch = [pltpu.VMEM((1, LANES), F32)]
        kernel = _inproj_fox_kernel
    return pl.pallas_call(
        kernel,
        grid=(b, s // ROW_TILE),
        in_specs=in_specs,
        out_specs=out_specs,
        out_shape=out_shape,
        scratch_shapes=scratch,
        compiler_params=pltpu.CompilerParams(
            dimension_semantics=("parallel", "arbitrary"), vmem_limit_bytes=VMEM_LIMIT),
        name="inproj_fox" if forget is not None else "inproj_sb",
    )(*args)


def _projection_weights(w_in):
    q, k, v, z = (w_in[:, i * D_INNER:(i + 1) * D_INNER] for i in range(4))
    return (jnp.concatenate([k, z], axis=1).astype(BF16),
            jnp.concatenate([q, v], axis=1).T.astype(BF16))


def _mask_keys_to_head(k, h):
    lane = lax.broadcasted_iota(jnp.int32, k.shape, 1)
    return jnp.where(lane // HEAD_DIM == h, k, jnp.zeros_like(k))


def _finalize(z_ref, o_ref, acc_ref, nq, normalize):
    for i in range(nq):
        outs = []
        for h in range(HEADS_PER_BLOCK):
            acc = acc_ref[i, h]
            o = acc[:HEAD_DIM]
            if normalize:
                o = o / acc[HEAD_DIM:HEAD_DIM + 1]
            outs.append(o)
        o = jnp.concatenate(outs, axis=0).T
        z = z_ref[0, i * TQ:(i + 1) * TQ, :].astype(F32)
        o_ref[0, i * TQ:(i + 1) * TQ, :] = (o * (z * jax.nn.sigmoid(z))).astype(o_ref.dtype)


def _wave_pair(n, nq):
    n_pairs = nq * (nq + 1) // 2
    starts = [e * nq - e * (e - 1) // 2 for e in range(nq)]
    if isinstance(n, int):
        n = min(max(n, 0), n_pairs - 1)
        d = max(e for e in range(nq) if starts[e] <= n)
        return n - starts[d] + d, n - starts[d]
    n = jnp.clip(n, 0, n_pairs - 1)
    d = jnp.int32(0)
    for e in range(1, nq):
        d = d + (n >= starts[e]).astype(jnp.int32)
    qi = n - (d * nq - d * (d - 1) // 2) + d
    return qi, qi - d


def _pick(pred, a, b):
    if isinstance(pred, bool):
        return a if pred else b
    return jnp.where(pred, a, b)


def _fox_kernel(qt_ref, k_ref, bp_ref, vt_ref, z_ref, o_ref,
                kx_ref, w_ref, vx_ref, mk_ref, s0_ref, s1_ref, p0_ref, p1_ref, m_ref, acc_ref, bnd_ref):
    s_len = k_ref.shape[1]
    nq = s_len // TQ
    n_pairs = nq * (nq + 1) // 2
    hp = pl.program_id(1)
    k = k_ref[0]
    bp = bp_ref[0]
    lane = lax.broadcasted_iota(jnp.int32, bp.shape, 1)
    for h in range(HEADS_PER_BLOCK):
        base = BIAS_STRIDE * hp + BIAS_PARTS * h
        kx_ref[h, :, :LANES] = _mask_keys_to_head(k, h)
        kx_ref[h, :, LANES:] = jnp.where((lane >= base) & (lane < base + BIAS_PARTS),
                                         bp, jnp.zeros_like(bp))
        vx_ref[h, :HEAD_DIM, :] = vt_ref[0, h * HEAD_DIM:(h + 1) * HEAD_DIM, :]
        vx_ref[h, HEAD_DIM:, :] = jnp.ones((VROWS - HEAD_DIM, s_len), BF16)
    w_ref[:LANES, :] = qt_ref[0]
    w_ref[LANES:, :] = jnp.ones((LANES, s_len), BF16)
    k_abs = jnp.max(jnp.abs(k.astype(F32)), axis=0, keepdims=True)
    lane1 = lax.broadcasted_iota(jnp.int32, (8, LANES), 1)
    k_abs = jnp.concatenate([jnp.where(lane1 // HEAD_DIM == h, k_abs, 0.0)
                             for h in range(HEADS_PER_BLOCK)], axis=0).astype(BF16)
    bound = jnp.dot(k_abs, jnp.abs(qt_ref[0]), preferred_element_type=F32)
    for h in range(HEADS_PER_BLOCK):
        bnd_ref[h] = bound[8 * h:8 * h + 1]
    key = lax.broadcasted_iota(jnp.int32, (TK, TQ), 0)
    qry = lax.broadcasted_iota(jnp.int32, (TK, TQ), 1)
    mk_ref[...] = jnp.where(key <= qry, 0.0, MASKED_SCORE)
    s_bufs, p_bufs = (s0_ref, s1_ref), (p0_ref, p1_ref)

    def score_stage(n, s_ref):
        qi, kj = _wave_pair(n, nq)
        w = w_ref[:, pl.ds(qi * TQ, TQ)]
        for h in range(HEADS_PER_BLOCK):
            s_ref[h] = jnp.dot(kx_ref[h, pl.ds(kj * TK, TK), :], w, preferred_element_type=F32)

    def softmax_stage(n, s_ref, p_ref, last):
        qi, kj = _wave_pair(n, nq)
        first = qi == kj
        assert isinstance(first, bool) or n_static_steps >= nq
        m_out, al_out = [], []
        for h in range(HEADS_PER_BLOCK):
            if first is True:
                st = s_ref[h] + mk_ref[...]
                m_new = jnp.max(st, axis=0, keepdims=True)
                al_out.append(None)
            else:
                st = s_ref[h]
                m = _pick(qi == last[0], last[1][h], m_ref[qi, h])
                m_new = jnp.maximum(m, jnp.max(st, axis=0, keepdims=True))
                al_out.append(jnp.exp2(m - m_new))
            p_ref[h] = jnp.exp2(st - m_new).astype(BF16)
            m_ref[qi, h] = m_new
            m_out.append(m_new)
        return (qi, m_out), al_out

    def value_stage(n, p_ref, al):
        qi, kj = _wave_pair(n, nq)
        for h in range(HEADS_PER_BLOCK):
            pv = jnp.dot(vx_ref[h, :, pl.ds(kj * TK, TK)], p_ref[h], preferred_element_type=F32)
            if (qi == kj) is True:
                acc_ref[qi, h] = pv
            elif isinstance(n, int):
                acc_ref[qi, h] = al[h] * acc_ref[qi, h] + pv
            else:
                acc = acc_ref[qi, h]
                acc_ref[qi, h] = jnp.where(n < n_pairs, al[h] * acc + pv, acc)

    def steps(it, state, unroll=UNROLL, base=0):
        al_prv, last = state
        for u in range(unroll):
            n = base + it * unroll + u
            par = (base + u) % 2
            score_stage(n + 1, s_bufs[1 - par])
            last, al = softmax_stage(n, s_bufs[par], p_bufs[par], last)
            if not (isinstance(n, int) and n == 0):
                value_stage(n - 1, p_bufs[1 - par], al_prv)
            al_prv = al
        return al_prv, last

    n_static_steps = min(SKIP_DISTANCE * nq - SKIP_DISTANCE * (SKIP_DISTANCE - 1) // 2, n_pairs)
    n_dynamic = -(-(n_pairs - n_static_steps) // UNROLL)
    score_stage(0, s_bufs[0])
    al_prv, last = steps(0, (None, (-1, None)), n_static_steps)
    state = (al_prv, (jnp.int32(last[0]), last[1]))

    def live(carry):
        it = carry[0]
        qi0, kj0 = _wave_pair(n_static_steps + it * UNROLL, nq)
        nearest = qi0 - kj0
        worst = jnp.float32(-jnp.inf)
        for i in range(SKIP_DISTANCE, nq):
            last_rows = (jnp.maximum(i - nearest, 0) + 1) * TK - 16
            for h in range(HEADS_PER_BLOCK):
                bias_end = jnp.sum(kx_ref[h, pl.ds(last_rows, 16), LANES:][15:].astype(F32))
                gap = jnp.max(bnd_ref[h, :, i * TQ:(i + 1) * TQ] - m_ref[i, h])
                worst = jnp.maximum(worst, jnp.where(i >= nearest, bias_end + gap, -jnp.inf))
        return (it < n_dynamic) & (worst > -DEAD_GAP)

    def body(carry):
        it, state = carry
        return it + 1, steps(it, state, UNROLL, n_static_steps)

    it, (al_prv, _) = lax.while_loop(live, body, (jnp.int32(0), state))
    value_stage(n_static_steps + it * UNROLL - 1, p_bufs[(n_static_steps - 1) % 2], al_prv)
    _finalize(z_ref, o_ref, acc_ref, nq, True)


def _sb_kernel(qt_ref, k_ref, vt_ref, z_ref, tri_ref, o_ref,
               kx_ref, mk_ref, s0_ref, s1_ref, lk0_ref, lk1_ref, lb0_ref, lb1_ref,
               p0_ref, p1_ref, r_ref, acc_ref):
    s_len = k_ref.shape[1]
    nq = s_len // TQ
    n_pairs = nq * (nq + 1) // 2
    k = k_ref[0]
    for h in range(HEADS_PER_BLOCK):
        kx_ref[h] = _mask_keys_to_head(k, h)
    key = lax.broadcasted_iota(jnp.int32, (TK, TQ), 0)
    qry = lax.broadcasted_iota(jnp.int32, (TK, TQ), 1)
    mk_ref[...] = jnp.where(key < qry, 0.0, MASKED_SCORE)
    s_bufs, lk_bufs, lb_bufs, p_bufs = ((s0_ref, s1_ref), (lk0_ref, lk1_ref),
                                        (lb0_ref, lb1_ref), (p0_ref, p1_ref))

    pair = lambda n: _wave_pair(n, nq)

    def score_stage(n, s_ref):
        qi, kj = pair(n)
        w = qt_ref[0, :, pl.ds(qi * TQ, TQ)]
        for h in range(HEADS_PER_BLOCK):
            s_ref[h] = jnp.dot(kx_ref[h, pl.ds(kj * TK, TK), :], w, preferred_element_type=F32)

    def gates(zz):
        neg_abs = lax.bitcast_convert_type(
            lax.bitcast_convert_type(zz, jnp.uint32) | jnp.uint32(0x80000000), F32)
        lb = jnp.minimum(zz, 0.0) - jnp.log2(1.0 + jnp.exp2(neg_abs))
        return lb, (lb - zz).astype(BF16)

    half = TK // 2
    live_blocks = ((slice(0, half), slice(0, TQ)), (slice(half, TK), slice(half, TQ)))

    def gate_stage(n, s_ref, lk_ref, lb_ref):
        qi, kj = pair(n)
        assert isinstance(qi, int) or n_static_steps >= nq
        for h in range(HEADS_PER_BLOCK):
            if (qi == kj) is True:
                for rows, cols in live_blocks:
                    lb, lk = gates(s_ref[h, rows, cols] + mk_ref[rows, cols])
                    lb_ref[h, rows, cols] = lb
                    lk_ref[h, rows, cols] = lk
                lk_ref[h, half:, :half] = jnp.zeros((TK - half, half), BF16)
            else:
                lb_ref[h], lk_ref[h] = gates(s_ref[h])

    def weight_stage(n, lk_ref, lb_ref, p_ref):
        qi, kj = pair(n)
        tot = []
        for h in range(HEADS_PER_BLOCK):
            after = jnp.dot(tri_ref[...], lk_ref[h], preferred_element_type=F32)
            if (qi == kj) is True:
                for rows, cols in live_blocks:
                    p_ref[h, rows, cols] = jnp.exp2(lb_ref[h, rows, cols] + after[rows, cols]).astype(BF16)
                p_ref[h, half:, :half] = jnp.zeros((TK - half, half), BF16)
            else:
                p_ref[h] = jnp.exp2(lb_ref[h] + after[:TK]).astype(BF16)
            tot.append(after[TK:TK + 1])
        return tot

    def value_stage(n, p_ref, tot, last):
        qi, kj = pair(n)
        r_out = []
        for h in range(HEADS_PER_BLOCK):
            pv = jnp.dot(vt_ref[0, h * HEAD_DIM:(h + 1) * HEAD_DIM, pl.ds(kj * TK, TK)], p_ref[h],
                         preferred_element_type=F32)
            if (qi == kj) is True:
                acc_ref[qi, h] = pv
                r_new = tot[h]
            else:
                r = _pick(qi == last[0], last[1][h], r_ref[qi, h])
                if isinstance(n, int):
                    acc_ref[qi, h] += jnp.exp2(r) * pv
                    r_new = r + tot[h]
                else:
                    ok = (n >= n_static_steps) & (n < n_pairs)
                    acc_ref[qi, h] += jnp.where(ok, jnp.exp2(r) * pv, 0.0)
                    r_new = jnp.where(ok, r + tot[h], r)
            r_ref[qi, h] = r_new
            r_out.append(r_new)
        return qi, r_out

    def steps(it, state, unroll=UNROLL, base=0):
        tot_prv, last = state
        for u in range(unroll):
            n = base + it * unroll + u
            par = (base + u) % 2
            score_stage(n + 2, s_bufs[par])
            gate_stage(n + 1, s_bufs[1 - par], lk_bufs[1 - par], lb_bufs[1 - par])
            tot = weight_stage(n, lk_bufs[par], lb_bufs[par], p_bufs[par])
            if not (isinstance(n, int) and n == 0):
                last = value_stage(n - 1, p_bufs[1 - par], tot_prv, last)
            tot_prv = tot
        return tot_prv, last

    n_static_steps = min(2 * nq - 1, n_pairs)
    n_dynamic = -(-(n_pairs - n_static_steps) // UNROLL)
    score_stage(0, s_bufs[0])
    gate_stage(0, s_bufs[0], lk_bufs[0], lb_bufs[0])
    score_stage(1, s_bufs[1])

    tot_prv, last = steps(0, (None, (-1, None)), n_static_steps)
    last = value_stage(n_static_steps - 1, p_bufs[(n_static_steps - 1) % 2], tot_prv, last)
    state = (tot_prv, (jnp.int32(last[0]), last[1]))

    def live(carry):
        it = carry[0]
        if nq <= FIRST_LIVE_TILE:
            return it < 0
        return (it < n_dynamic) & (jnp.max(jnp.exp2(r_ref[FIRST_LIVE_TILE:])) > 0.0)

    def body(carry):
        it, state = carry
        return it + 1, steps(it, state, UNROLL, n_static_steps)

    it, (tot_prv, last) = lax.while_loop(live, body, (jnp.int32(0), state))
    value_stage(n_static_steps + it * UNROLL - 1, p_bufs[(n_static_steps - 1) % 2], tot_prv, last)
    _finalize(z_ref, o_ref, acc_ref, nq, False)


def _mixer_call(kernel, name, args, in_specs, scratch, b, s):
    seq_spec = pl.BlockSpec((1, s, LANES), lambda i, j: (i, 0, j))
    return pl.pallas_call(
        kernel,
        grid=(b, D_INNER // LANES),
        in_specs=in_specs,
        out_specs=seq_spec,
        out_shape=jax.ShapeDtypeStruct((b, s, D_INNER), BF16),
        scratch_shapes=scratch,
        compiler_params=pltpu.CompilerParams(
            dimension_semantics=("parallel", "parallel"), vmem_limit_bytes=VMEM_LIMIT),
        name=name,
    )(*args)


def _fox_mixer(qt, k, bp, vt, z):
    b, s, _ = k.shape
    nq = s // TQ
    assert UNROLL % 2 == 0
    seq_spec = pl.BlockSpec((1, s, LANES), lambda i, j: (i, 0, j))
    t_spec = pl.BlockSpec((1, LANES, s), lambda i, j: (i, j, 0))
    bp_spec = pl.BlockSpec((1, s, LANES), lambda i, j: (i, 0, 0))
    tile = (HEADS_PER_BLOCK, TK, TQ)
    scratch = [pltpu.VMEM((HEADS_PER_BLOCK, s, 2 * LANES), BF16),
               pltpu.VMEM((2 * LANES, s), BF16),
               pltpu.VMEM((HEADS_PER_BLOCK, VROWS, s), BF16),
               pltpu.VMEM((TK, TQ), F32),
               pltpu.VMEM(tile, F32), pltpu.VMEM(tile, F32),
               pltpu.VMEM(tile, BF16), pltpu.VMEM(tile, BF16),
               pltpu.VMEM((nq, HEADS_PER_BLOCK, 1, TQ), F32),
               pltpu.VMEM((nq, HEADS_PER_BLOCK, VROWS, TQ), F32),
               pltpu.VMEM((HEADS_PER_BLOCK, 1, s), F32)]
    return _mixer_call(_fox_kernel, "fox", (qt, k, bp, vt, z),
                       [t_spec, seq_spec, bp_spec, t_spec, seq_spec], scratch, b, s)


def _sb_mixer(qt, k, vt, z):
    b, s, _ = k.shape
    nq = s // TQ
    assert UNROLL % 2 == 0
    r = lax.broadcasted_iota(jnp.int32, (TRI_ROWS, TK), 0)
    c = lax.broadcasted_iota(jnp.int32, (TRI_ROWS, TK), 1)
    tri = ((c > r) | (r >= TK)).astype(BF16)
    seq_spec = pl.BlockSpec((1, s, LANES), lambda i, j: (i, 0, j))
    t_spec = pl.BlockSpec((1, LANES, s), lambda i, j: (i, j, 0))
    tri_spec = pl.BlockSpec((TRI_ROWS, TK), lambda i, j: (0, 0))
    tile = (HEADS_PER_BLOCK, TK, TQ)
    scratch = [pltpu.VMEM((HEADS_PER_BLOCK, s, LANES), BF16),
               pltpu.VMEM((TK, TQ), F32),
               pltpu.VMEM(tile, F32), pltpu.VMEM(tile, F32),
               pltpu.VMEM(tile, BF16), pltpu.VMEM(tile, BF16),
               pltpu.VMEM(tile, F32), pltpu.VMEM(tile, F32),
               pltpu.VMEM(tile, BF16), pltpu.VMEM(tile, BF16),
               pltpu.VMEM((nq, HEADS_PER_BLOCK, 1, TQ), F32),
               pltpu.VMEM((nq, HEADS_PER_BLOCK, HEAD_DIM, TQ), F32)]
    return _mixer_call(_sb_kernel, "sb", (qt, k, vt, z, tri),
                       [t_spec, seq_spec, t_spec, seq_spec, tri_spec], scratch, b, s)


def _outproj_kernel(u_ref, w_ref, x_ref, gate_ref, o_ref):
    y = jnp.dot(u_ref[0], w_ref[...], preferred_element_type=F32)
    o_ref[0] = x_ref[0] + gate_ref[0] * y


def _outproj_final_kernel(u_ref, w_ref, x_ref, gate_ref, g_ref, o_ref):
    y = jnp.dot(u_ref[0], w_ref[...], preferred_element_type=F32)
    r = x_ref[0] + gate_ref[0] * y
    o_ref[0] = r * lax.rsqrt(jnp.mean(r * r, axis=-1, keepdims=True) + NORM_EPS) * g_ref[...]


def _outproj(u, w, x, mod3, final_g=None):
    b, s, d = x.shape
    row_spec = lambda width: pl.BlockSpec((1, ROW_TILE, width), lambda i, j: (i, j, 0))
    const = lambda shape: pl.BlockSpec(shape, lambda i, j: (0,) * len(shape),
                                       pipeline_mode=pl.Buffered(1))
    in_specs = [row_spec(D_INNER), const(w.shape), row_spec(d),
                pl.BlockSpec((1, 1, d), lambda i, j: (3 * i + 2, 0, 0))]
    args = [u, w, x, mod3]
    kernel = _outproj_kernel
    if final_g is not None:
        in_specs.append(const((1, d)))
        args.append(final_g.reshape(1, d))
        kernel = _outproj_final_kernel
    return pl.pallas_call(
        kernel,
        grid=(b, s // ROW_TILE),
        in_specs=in_specs,
        out_specs=row_spec(d),
        out_shape=jax.ShapeDtypeStruct((b, s, d), F32),
        compiler_params=pltpu.CompilerParams(
            dimension_semantics=("parallel", "parallel"), vmem_limit_bytes=VMEM_LIMIT),
        name="outproj_final" if final_g is not None else "outproj",
    )(*args)


def _bias_lane_permutations():
    src = lax.broadcasted_iota(jnp.int32, (BIAS_PARTS, LANES, LANES), 1)
    dst = lax.broadcasted_iota(jnp.int32, (BIAS_PARTS, LANES, LANES), 2)
    part = lax.broadcasted_iota(jnp.int32, (BIAS_PARTS, LANES, LANES), 0)
    want = BIAS_STRIDE * (src // HEADS_PER_BLOCK) + BIAS_PARTS * (src % HEADS_PER_BLOCK) + part
    return ((dst == want) & (src < N_HEADS)).astype(BF16)


def _fox_layer(x, c, norm_g, w_ada, b_ada, w_in, b_f, w_out, final_g):
    b, s, d = x.shape
    mod3 = _adaln(c, w_ada, b_ada).reshape(b * 3, 1, d)
    wkz, wqvt = _projection_weights(w_in)
    wf = jnp.pad(w_in[:, 4 * D_INNER:], ((0, 0), (0, LANES - N_HEADS))).astype(BF16)
    bf = jnp.pad(b_f, (0, LANES - N_HEADS)).reshape(1, LANES)
    qt, k, vt, z, bp = _inproj(x, mod3, norm_g, wkz, wqvt, forget=(wf, bf, _bias_lane_permutations()))
    u = _fox_mixer(qt, k, bp, vt, z)
    return _outproj(u, w_out.astype(BF16), x, mod3, final_g)


def _sb_layer(x, c, norm_g, w_ada, b_ada, w_in, w_out, final_g):
    b, s, d = x.shape
    mod3 = _adaln(c, w_ada, b_ada).reshape(b * 3, 1, d)
    wkz, wqvt = _projection_weights(w_in)
    qt, k, vt, z = _inproj(x, mod3, norm_g, wkz, wqvt)
    u = _sb_mixer(qt, k, vt, z)
    return _outproj(u, w_out.astype(BF16), x, mod3, final_g)


def kernel(x, c, fox_norm_g, fox_w_ada, fox_b_ada, fox_w_in, fox_b_f, fox_w_out,
           sb_norm_g, sb_w_ada, sb_b_ada, sb_w_in, sb_w_out, final_norm_g):
    depth = fox_norm_g.shape[0] + sb_norm_g.shape[0]
    for i in range(depth):
        j = i // 2
        final_g = final_norm_g if i == depth - 1 else None
        if i % 2 == 0:
            x = _fox_layer(x, c, fox_norm_g[j], fox_w_ada[j], fox_b_ada[j],
                           fox_w_in[j], fox_b_f[j], fox_w_out[j], final_g)
        else:
            x = _sb_layer(x, c, sb_norm_g[j], sb_w_ada[j], sb_b_ada[j],
                          sb_w_in[j], sb_w_out[j], final_g)
    return x
```
